```python
import math
import jax
import jax.numpy as jnp
from jax import lax
import numpy as np

D_MODEL = 1024
BATCH = 8
SEQ = 2048
DEPTH = 4
DEC_BATCH = 16
DEC_SEQ = 4096
PAST_LEN = 128

N_MIXERS = 3
N_A = (DEPTH + 2) // 3
N_B = (DEPTH + 1) // 3
N_C = DEPTH // 3

EPS = 1e-6
D_FF = 2816

A_HEADS = 8
A_DK = 128
A_DV = 128
A_QK = A_HEADS * A_DK
A_V = A_HEADS * A_DV
A_CONV = 3
A_CHUNK = 64
A_PROJ = 2 * A_QK + 2 * A_V + 4 * A_HEADS

B_HEADS = 4
B_DK = 256
B_DV = 512
B_CHUNK = 128
ROPE_BASE = 10000.0
B_PROJ = 2 * B_HEADS * B_DK + 2 * B_HEADS * B_DV

C_ORDER = 2
C_CONV = 3
C_EMB = 33
C_FILTER_WIDTH = 64
C_TARGET = 1e-2
C_SHORT_DECAY_PCT = 0.3
C_LONG_DECAY_PCT = 1.5
C_MIN_DECAY = math.log(C_TARGET) / C_LONG_DECAY_PCT
C_MAX_DECAY = math.log(C_TARGET) / C_SHORT_DECAY_PCT

kernel_name = "hybrid_bidir_deltanet_retnet_hyena_encoder"

F32 = jnp.float32


def rmsnorm(x, g):
    xf = x.astype(F32)
    y = xf * lax.rsqrt(jnp.mean(xf * xf, axis=-1, keepdims=True) + EPS)
    return (y * g.astype(F32)).astype(x.dtype)


def l2norm(x):
    return x * lax.rsqrt(jnp.sum(x * x, axis=-1, keepdims=True) + EPS)


def short_conv(x, w):
    k = w.shape[0]
    return lax.conv_general_dilated(
        x, w[:, None, :].astype(x.dtype), window_strides=(1,), padding=[(k // 2, k // 2)],
        dimension_numbers=("NWC", "WIO", "NWC"), feature_group_count=x.shape[-1])


def swiglu(x, wg, wu, wd):
    return (jax.nn.silu(x @ wg) * (x @ wu)) @ wd


def flip_seq(t):
    return jnp.flip(t, axis=2)


def gated_delta_rule(q, k, v, g, beta):
    bsz, nh, L, dk = k.shape
    dv = v.shape[-1]
    c = A_CHUNK
    n = L // c
    q, k, v = (t.reshape(bsz, nh, n, c, t.shape[-1]) for t in (q, k, v))
    g = jnp.cumsum(g.reshape(bsz, nh, n, c), axis=-1)
    beta = beta.reshape(bsz, nh, n, c)
    causal = jnp.tril(jnp.ones((c, c), dtype=bool))
    strict = jnp.tril(jnp.ones((c, c), dtype=bool), -1)
    diff = g[..., :, None] - g[..., None, :]
    decay = jnp.where(causal, jnp.exp(jnp.where(causal, diff, 0.0)), 0.0)
    k_beta = k * beta[..., None]
    a = jnp.where(strict, jnp.einsum("bhncd,bhnmd->bhncm", k_beta, k) * decay, 0.0)
    eye = jnp.eye(c, dtype=F32)
    t_inv = lax.linalg.triangular_solve(a + eye, jnp.broadcast_to(eye, a.shape), left_side=True,
                                        lower=True, unit_diagonal=True)
    u = jnp.einsum("bhncm,bhnme->bhnce", t_inv, v * beta[..., None])
    w = jnp.einsum("bhncm,bhnmd->bhncd", t_inv, k_beta * jnp.exp(g)[..., None])
    attn = jnp.where(causal, jnp.einsum("bhncd,bhnmd->bhncm", q, k) * decay, 0.0)
    q_dec = q * jnp.exp(g)[..., None]
    k_dec = k * jnp.exp(g[..., -1:] - g)[..., None]
    g_end = jnp.exp(g[..., -1])
    xs = tuple(jnp.moveaxis(t, 2, 0) for t in (u, w, attn, q_dec, k_dec, g_end))

    def step(s, inp):
        u_c, w_c, attn_c, qd_c, kd_c, ge_c = inp
        v_new = u_c - jnp.einsum("bhcd,bhde->bhce", w_c, s)
        o_c = jnp.einsum("bhcd,bhde->bhce", qd_c, s) + jnp.einsum("bhcm,bhme->bhce", attn_c, v_new)
        s = s * ge_c[..., None, None] + jnp.einsum("bhcd,bhce->bhde", kd_c, v_new)
        return s, o_c

    s0 = jnp.zeros((bsz, nh, dk, dv), F32)
    _, o = lax.scan(step, s0, xs)
    return jnp.moveaxis(o, 0, 2).reshape(bsz, nh, L, dv)


def gdn_mixer(x, w_in, conv_w, a_log, dt_bias, norm_w, w_out):
    bsz, L, _ = x.shape
    p = x @ w_in
    n_qkv = 2 * A_QK + A_V
    qkv = jax.nn.silu(short_conv(p[..., :n_qkv], conv_w)).astype(F32)
    z = p[..., n_qkv:n_qkv + A_V].astype(F32).reshape(bsz, L, A_HEADS, A_DV)
    ab = p[..., n_qkv + A_V:].astype(F32).reshape(bsz, L, 2, 2, A_HEADS)

    def heads(t, d):
        return t.reshape(bsz, L, A_HEADS, d).transpose(0, 2, 1, 3)

    q = l2norm(heads(qkv[..., :A_QK], A_DK)) * (A_DK ** -0.5)
    k = l2norm(heads(qkv[..., A_QK:2 * A_QK], A_DK))
    v = heads(qkv[..., 2 * A_QK:], A_DV)
    g = -jnp.exp(a_log.astype(F32)) * jax.nn.softplus(ab[:, :, :, 0] + dt_bias.astype(F32))
    beta = jax.nn.sigmoid(ab[:, :, :, 1])
    g = g.transpose(2, 0, 3, 1)
    beta = beta.transpose(2, 0, 3, 1)
    o_fwd = gated_delta_rule(q, k, v, g[0], beta[0])
    o_bwd = flip_seq(gated_delta_rule(flip_seq(q), flip_seq(k), flip_seq(v), flip_seq(g[1]), flip_seq(beta[1])))
    o = (o_fwd + o_bwd).transpose(0, 2, 1, 3)
    o = o * lax.rsqrt(jnp.mean(o * o, axis=-1, keepdims=True) + EPS) * norm_w.astype(F32) * jax.nn.silu(z)
    return o.reshape(bsz, L, A_V).astype(x.dtype) @ w_out


def rope(x):
    L, d = x.shape[2], x.shape[3]
    inv = ROPE_BASE ** (-jnp.arange(0, d, 2, dtype=F32) / d)
    ang = jnp.arange(L, dtype=F32)[:, None] * inv[None, :]
    cos, sin = jnp.cos(ang), jnp.sin(ang)
    x1, x2 = x[..., :d // 2], x[..., d // 2:]
    return jnp.concatenate([x1 * cos - x2 * sin, x2 * cos + x1 * sin], axis=-1)


def retention_chunked(q, k, v, log_gamma):
    bsz, nh, L, dk = q.shape
    dv = v.shape[-1]
    c = B_CHUNK
    n = L // c
    pos = jnp.arange(c, dtype=F32)
    dist = pos[:, None] - pos[None, :]
    dmask = jnp.where(dist >= 0, jnp.exp(log_gamma[:, None, None] * jnp.maximum(dist, 0.0)), 0.0)
    zeta = jnp.exp(log_gamma[:, None] * (c - 1 - pos))
    xi = jnp.exp(log_gamma[:, None] * (pos + 1))
    g_chunk = jnp.exp(log_gamma * c)
    q, k, v = (t.reshape(bsz, nh, n, c, t.shape[-1]) for t in (q, k, v))
    scores = jnp.einsum("bhncd,bhnmd->bhncm", q, k) * dmask[None, :, None]
    inner = jnp.einsum("bhncm,bhnme->bhnce", scores, v)
    qx = q * xi[None, :, None, :, None]
    kz = k * zeta[None, :, None, :, None]
    xs = (jnp.moveaxis(qx, 2, 0), jnp.moveaxis(kz, 2, 0), jnp.moveaxis(v, 2, 0))

    def step(r, inp):
        qx_c, kz_c, v_c = inp
        cross = jnp.einsum("bhcd,bhde->bhce", qx_c, r)
        r = r * g_chunk[None, :, None, None] + jnp.einsum("bhcd,bhce->bhde", kz_c, v_c)
        return r, cross

    r0 = jnp.zeros((bsz, nh, dk, dv), F32)
    _, cross = lax.scan(step, r0, xs)
    return (inner + jnp.moveaxis(cross, 0, 2)).reshape(bsz, nh, L, dv)


def retnet_mixer(x, w_in, decay_logit, gn_w, w_out):
    bsz, L, _ = x.shape
    p = x @ w_in
    nq = B_HEADS * B_DK
    nv = B_HEADS * B_DV

    def heads(t, d):
        return t.astype(F32).reshape(bsz, L, B_HEADS, d).transpose(0, 2, 1, 3)

    q = rope(heads(p[..., :nq], B_DK))
    k = rope(heads(p[..., nq:2 * nq], B_DK)) * (B_DK ** -0.5)
    v = heads(p[..., 2 * nq:2 * nq + nv], B_DV)
    gate = p[..., 2 * nq + nv:].astype(F32)
    log_gamma = jnp.log1p(-jnp.exp2(decay_logit.astype(F32)))
    o = retention_chunked(q, k, v, log_gamma[0]) + flip_seq(
        retention_chunked(flip_seq(q), flip_seq(k), flip_seq(v), log_gamma[1]))
    o = o.transpose(0, 2, 1, 3)
    mu = jnp.mean(o, axis=-1, keepdims=True)
    oc = o - mu
    on = oc * lax.rsqrt(jnp.mean(oc * oc, axis=-1, keepdims=True) + EPS)
    on = on.reshape(bsz, L, nv) * gn_w.astype(F32)
    return (jax.nn.silu(gate) * on).astype(x.dtype) @ w_out


def hyena_filters(L, w1, b1, fr1, w2, b2, fr2, w3):
    bands = (C_EMB - 1) // 2
    t = jnp.linspace(0.0, 1.0, L, dtype=F32)[:, None]
    omega = 2.0 * math.pi * jnp.arange(L, dtype=F32)[:, None] / L
    f = jnp.linspace(1e-4, bands - 1, bands, dtype=F32)[None, :]
    z = jnp.concatenate([t, jnp.cos(f * omega), -jnp.sin(f * omega)], axis=-1)
    h = jnp.sin(fr1.astype(F32) * (z @ w1.astype(F32) + b1.astype(F32)))
    h = jnp.sin(fr2.astype(F32) * (h @ w2.astype(F32) + b2.astype(F32)))
    h = h @ w3.astype(F32)
    deltas = jnp.abs(jnp.linspace(C_MIN_DECAY, C_MAX_DECAY, D_MODEL, dtype=F32))
    window = jnp.exp(-t * deltas[None, :])
    return h.reshape(L, 2, C_ORDER, D_MODEL) * window[:, None, None, :]


def bidir_long_conv(u, h_fwd, h_bwd, skip):
    L, d = h_fwd.shape
    filt = jnp.concatenate([h_fwd, jnp.zeros((1, d), F32), jnp.flip(h_bwd[1:], axis=0)], axis=0)
    filt_f = jnp.fft.rfft(filt, axis=0)
    u_f = jnp.fft.rfft(u, n=2 * L, axis=1)
    y = jnp.fft.irfft(u_f * filt_f[None], n=2 * L, axis=1)[:, :L]
    return y + u * skip


def hyena_mixer(x, w_in, b_in, conv_w, conv_b, f_w1, f_b1, f_fr1, f_w2, f_b2, f_fr2, f_w3, bias_d, w_out, b_out):
    L = x.shape[1]
    p = (short_conv(x @ w_in + b_in, conv_w) + conv_b).astype(F32)
    v, g1, g2 = p[..., :D_MODEL], p[..., D_MODEL:2 * D_MODEL], p[..., 2 * D_MODEL:]
    h = hyena_filters(L, f_w1, f_b1, f_fr1, f_w2, f_b2, f_fr2, f_w3)
    z = v
    for n, gate in enumerate((g1, g2)):
        z = gate * bidir_long_conv(z, h[:, 0, n], h[:, 1, n], bias_d[n].astype(F32))
    return z.astype(x.dtype) @ w_out + b_out


def setup_inputs(seed: int = 0) -> dict:
    key = jax.random.key(seed)
    ks = iter(jax.random.split(key, 40))

    def nrm(shape, scale):
        return jax.random.normal(next(ks), shape, F32) * scale

    def gain(shape):
        return 1.0 + nrm(shape, 0.05)

    x_prompt = nrm((BATCH, SEQ, D_MODEL), 1.0)
    x_sample = nrm((DEC_BATCH, DEC_SEQ, D_MODEL), 1.0)
    norm_g = gain((DEPTH, 6, D_MODEL))
    ffn_w_gate = nrm((DEPTH, 2, D_MODEL, D_FF), D_MODEL ** -0.5)
    ffn_w_up = nrm((DEPTH, 2, D_MODEL, D_FF), D_MODEL ** -0.5)
    ffn_w_down = nrm((DEPTH, 2, D_FF, D_MODEL), D_FF ** -0.5)
    a_w_in = nrm((N_A, D_MODEL, A_PROJ), D_MODEL ** -0.5)
    a_conv_w = nrm((N_A, A_CONV, 2 * A_QK + A_V), A_CONV ** -0.5)
    a_a_log = jnp.log(jax.random.uniform(next(ks), (N_A, 2, A_HEADS), F32, 1.0, 16.0))
    dt = jnp.exp(jax.random.uniform(next(ks), (N_A, 2, A_HEADS), F32, math.log(1e-3), math.log(1e-1)))
    a_dt_bias = dt + jnp.log(-jnp.expm1(-dt))
    a_norm_w = gain((N_A, A_DV))
    a_w_out = nrm((N_A, A_V, D_MODEL), A_V ** -0.5)
    b_w_in = nrm((N_B, D_MODEL, B_PROJ), D_MODEL ** -0.5)
    b_decay_logit = -(5.0 + jnp.arange(B_HEADS, dtype=F32))[None, None, :] + nrm((N_B, 2, B_HEADS), 0.1)
    b_gn_w = gain((N_B, B_HEADS * B_DV))
    b_w_out = nrm((N_B, B_HEADS * B_DV, D_MODEL), (B_HEADS * B_DV) ** -0.5)
    c_w_in = nrm((N_C, D_MODEL, 3 * D_MODEL), D_MODEL ** -0.5)
    c_b_in = nrm((N_C, 3 * D_MODEL), 0.02)
    c_conv_w = nrm((N_C, C_CONV, 3 * D_MODEL), C_CONV ** -0.5)
    c_conv_b = nrm((N_C, 3 * D_MODEL), 0.02)
    c_f_w1 = nrm((N_C, C_EMB, C_FILTER_WIDTH), C_EMB ** -0.5)
    c_f_b1 = nrm((N_C, C_FILTER_WIDTH), 0.1)
    c_f_freq1 = gain((N_C, C_FILTER_WIDTH))
    c_f_w2 = nrm((N_C, C_FILTER_WIDTH, C_FILTER_WIDTH), C_FILTER_WIDTH ** -0.5)
    c_f_b2 = nrm((N_C, C_FILTER_WIDTH), 0.1)
    c_f_freq2 = gain((N_C, C_FILTER_WIDTH))
    c_f_w3 = nrm((N_C, C_FILTER_WIDTH, 2 * C_ORDER * D_MODEL), C_FILTER_WIDTH ** -0.5)
    c_bias_d = nrm((N_C, C_ORDER, D_MODEL), 1.0)
    c_w_out = nrm((N_C, D_MODEL, D_MODEL), D_MODEL ** -0.5)
    c_b_out = nrm((N_C, D_MODEL), 0.02)
    return {
        "x_prompt": x_prompt, "x_sample": x_sample, "norm_g": norm_g,
        "ffn_w_gate": ffn_w_gate, "ffn_w_up": ffn_w_up, "ffn_w_down": ffn_w_down,
        "a_w_in": a_w_in, "a_conv_w": a_conv_w, "a_a_log": a_a_log, "a_dt_bias": a_dt_bias,
        "a_norm_w": a_norm_w, "a_w_out": a_w_out,
        "b_w_in": b_w_in, "b_decay_logit": b_decay_logit, "b_gn_w": b_gn_w, "b_w_out": b_w_out,
        "c_w_in": c_w_in, "c_b_in": c_b_in, "c_conv_w": c_conv_w, "c_conv_b": c_conv_b,
        "c_f_w1": c_f_w1, "c_f_b1": c_f_b1, "c_f_freq1": c_f_freq1,
        "c_f_w2": c_f_w2, "c_f_b2": c_f_b2, "c_f_freq2": c_f_freq2, "c_f_w3": c_f_w3,
        "c_bias_d": c_bias_d, "c_w_out": c_w_out, "c_b_out": c_b_out,
    }


def reference(x_prompt, x_sample, norm_g, ffn_w_gate, ffn_w_up, ffn_w_down,
              a_w_in, a_conv_w, a_a_log, a_dt_bias, a_norm_w, a_w_out,
              b_w_in, b_decay_logit, b_gn_w, b_w_out,
              c_w_in, c_b_in, c_conv_w, c_conv_b, c_f_w1, c_f_b1, c_f_freq1,
              c_f_w2, c_f_b2, c_f_freq2, c_f_w3, c_bias_d, c_w_out, c_b_out):

    def token_mixer(h, i):
        kind, j = i % N_MIXERS, i // N_MIXERS
        if kind == 0:
            return gdn_mixer(h, a_w_in[j], a_conv_w[j], a_a_log[j], a_dt_bias[j], a_norm_w[j], a_w_out[j])
        if kind == 1:
            return retnet_mixer(h, b_w_in[j], b_decay_logit[j], b_gn_w[j], b_w_out[j])
        return hyena_mixer(h, c_w_in[j], c_b_in[j], c_conv_w[j], c_conv_b[j], c_f_w1[j], c_f_b1[j],
                           c_f_freq1[j], c_f_w2[j], c_f_b2[j], c_f_freq2[j], c_f_w3[j], c_bias_d[j],
                           c_w_out[j], c_b_out[j])

    def trunk(x):
        for i in range(DEPTH):
            g = norm_g[i]
            h = swiglu(rmsnorm(x, g[0]), ffn_w_gate[i, 0], ffn_w_up[i, 0], ffn_w_down[i, 0])
            x = x + 0.5 * rmsnorm(h, g[1])
            h = token_mixer(rmsnorm(x, g[2]), i)
            x = x + rmsnorm(h, g[3])
            h = swiglu(rmsnorm(x, g[4]), ffn_w_gate[i, 1], ffn_w_up[i, 1], ffn_w_down[i, 1])
            x = x + 0.5 * rmsnorm(h, g[5])
        return x

    y_prompt = trunk(x_prompt)
    y_sample = trunk(x_sample)
    return (y_prompt, y_sample)
```

```python
import functools
import math

import numpy as np
import jax
import jax.numpy as jnp
from jax import lax
from jax.experimental import pallas as pl
from jax.experimental.pallas import tpu as pltpu

F32 = jnp.float32
BF16 = jnp.bfloat16
EPS = 1e-6

D_MODEL = 1024
D_FF = 2816

A_HEADS, A_DK, A_DV, A_CHUNK = 8, 128, 128, 64
A_QK = A_HEADS * A_DK
A_V = A_HEADS * A_DV
B_HEADS, B_DK, B_DV = 4, 256, 512
ROPE_BASE = 10000.0
C_EMB, C_FILTER_WIDTH = 33, 64
C_TARGET, C_SHORT_DECAY_PCT, C_LONG_DECAY_PCT = 1e-2, 0.3, 1.5
C_MIN_DECAY = math.log(C_TARGET) / C_LONG_DECAY_PCT
C_MAX_DECAY = math.log(C_TARGET) / C_SHORT_DECAY_PCT

V7X_VMEM_BYTES = 64 * 1024 * 1024
VMEM_LIMIT = V7X_VMEM_BYTES - 8 * 1024 * 1024
SUBLANE = 8
LANE = 128
MXU_DIM = 256

HALO = SUBLANE


def _params(*sem):
    return pltpu.CompilerParams(dimension_semantics=sem, vmem_limit_bytes=VMEM_LIMIT)


def _const_spec(shape):
    nd = len(shape)
    return pl.BlockSpec(shape, lambda *_: (0,) * nd, pipeline_mode=pl.Buffered(1))


def _rms(x, g):
    return x * lax.rsqrt(jnp.mean(x * x, axis=-1, keepdims=True) + EPS) * g


def _silu(x):
    return x * jax.nn.sigmoid(x)


def _dot(a, b):
    return jnp.dot(a, b, preferred_element_type=F32)


def _dot_nt(a, b):
    return lax.dot_general(a, b, (((1,), (1,)), ((), ())), preferred_element_type=F32)


def _dot_tn(a, b):
    return lax.dot_general(a, b, (((0,), (0,)), ((), ())), preferred_element_type=F32)


FFN_TM = 512
FFN_CHUNKS = 2


def _ffn_kernel(x_ref, g_ref, wg_ref, wu_ref, wd_ref, o_ref):
    x = x_ref[...]
    g = g_ref[...]
    xn = _rms(x, g[0:1]).astype(BF16)
    fc = D_FF // FFN_CHUNKS
    y = None
    for c in range(FFN_CHUNKS):
        hg = _dot(xn, wg_ref[:, c * fc:(c + 1) * fc])
        hu = _dot(xn, wu_ref[:, c * fc:(c + 1) * fc])
        act = (_silu(hg) * hu).astype(BF16)
        part = _dot(act, wd_ref[c * fc:(c + 1) * fc, :])
        y = part if y is None else y + part
    o_ref[...] = x + 0.5 * _rms(y, g[1:2])


def ffn_block(x2, g_pair, wg, wu, wd):
    t = x2.shape[0]
    tm = FFN_TM
    return pl.pallas_call(
        _ffn_kernel,
        grid=(t // tm,),
        in_specs=[
            pl.BlockSpec((tm, D_MODEL), lambda i: (i, 0)),
            _const_spec((2, D_MODEL)),
            _const_spec((D_MODEL, D_FF)),
            _const_spec((D_MODEL, D_FF)),
            _const_spec((D_FF, D_MODEL)),
        ],
        out_specs=pl.BlockSpec((tm, D_MODEL), lambda i: (i, 0)),
        out_shape=jax.ShapeDtypeStruct((t, D_MODEL), F32),
        compiler_params=_params("parallel"),
        name="ffn_block",
    )(x2, g_pair, wg, wu, wd)


PROJ_TM = 512


def _halo_specs(tm, seq):
    per = tm // HALO
    last = seq // HALO - 1
    return [
        pl.BlockSpec((1, tm, D_MODEL), lambda b, i: (b, i, 0)),
        pl.BlockSpec((1, HALO, D_MODEL), lambda b, i: (b, jnp.maximum(i * per - 1, 0), 0)),
        pl.BlockSpec((1, HALO, D_MODEL), lambda b, i: (b, jnp.minimum((i + 1) * per, last), 0)),
    ]


def _normed_with_halo(xm_ref, xp_ref, xn_ref, g):
    xe = jnp.concatenate([xp_ref[0], xm_ref[0], xn_ref[0]], axis=0)
    return _rms(xe, g).astype(BF16)


def _halo_valid(tm, n_tiles):
    i = pl.program_id(1)
    row = lax.broadcasted_iota(jnp.int32, (tm + 2 * HALO, 1), 0)
    before = jnp.logical_and(row < HALO, i == 0)
    after = jnp.logical_and(row >= tm + HALO, i == n_tiles - 1)
    return jnp.logical_not(jnp.logical_or(before, after))


def _conv3(p, cw, tm):
    return (p[HALO - 1:HALO - 1 + tm] * cw[0:1] + p[HALO:HALO + tm] * cw[1:2]
            + p[HALO + 1:HALO + 1 + tm] * cw[2:3])


GATE_LANES = LANE


def _chunk_scan(val, pos, size, axis, reverse):
    s = 1
    while s < A_CHUNK:
        if reverse:
            shifted = pltpu.roll(val, size - s, axis)
            val = val + jnp.where(pos < A_CHUNK - s, shifted, 0.0)
        else:
            shifted = pltpu.roll(val, s, axis)
            val = val + jnp.where(pos >= s, shifted, 0.0)
        s *= 2
    return val


def _gdn_proj_kernel(xm_ref, xp_ref, xn_ref, g_ref, wqkv_ref, wz_ref, wab_ref, cw_ref, alog_ref, dtb_ref,
                     q_ref, k_ref, v_ref, z_ref, gcol_ref, grow_ref, *, tm, n_tiles):
    xne = _normed_with_halo(xm_ref, xp_ref, xn_ref, g_ref[...])
    valid = _halo_valid(tm, n_tiles)
    outs = (q_ref, k_ref, v_ref)
    for part in range(3):
        cols = slice(part * A_QK, (part + 1) * A_QK)
        p = jnp.where(valid, _dot(xne, wqkv_ref[:, cols]), 0.0)
        a = _silu(_conv3(p, cw_ref[:, cols], tm))
        if part == 2:
            v_ref[0] = a.astype(BF16)
            continue
        scale = A_DK ** -0.5 if part == 0 else 1.0
        for h in range(A_HEADS):
            ah = a[:, h * A_DK:(h + 1) * A_DK]
            inv = lax.rsqrt(jnp.sum(ah * ah, axis=-1, keepdims=True) + EPS) * scale
            outs[part][0, :, h * A_DK:(h + 1) * A_DK] = (ah * inv).astype(BF16)
    xn_main = xne[HALO:HALO + tm]
    z_ref[0] = _dot(xn_main, wz_ref[...]).astype(BF16)
    ab = _dot(xn_main, wab_ref[...])
    col = lax.broadcasted_iota(jnp.int32, (1, GATE_LANES), 1)
    is_decay = jnp.logical_and(col % 16 < 8, col < 32)
    sp = ab + dtb_ref[...]
    softplus = jnp.maximum(sp, 0.0) + jnp.log(1.0 + jnp.exp(-jnp.abs(sp)))
    val = jnp.where(is_decay, -jnp.exp(alog_ref[...]) * softplus, jax.nn.sigmoid(ab))
    pos = lax.broadcasted_iota(jnp.int32, (tm, 1), 0) % A_CHUNK
    fwd = _chunk_scan(val, pos, tm, 0, False)
    bwd = _chunk_scan(val, pos, tm, 0, True)
    gate = jnp.where(col < 8, fwd, jnp.where(jnp.logical_and(col >= 16, col < 24), bwd, val))
    gcol_ref[0] = gate
    grow_ref[0] = gate.T[0:4 * A_HEADS, :]


def gdn_project(x, g_pre, wqkv, wz, wab, conv_w, alog_row, dtb_row):
    bsz, seq, _ = x.shape
    tm = PROJ_TM
    nt = seq // tm
    act = lambda c: jax.ShapeDtypeStruct((bsz, seq, c), BF16)
    act_spec = lambda c: pl.BlockSpec((1, tm, c), lambda b, i: (b, i, 0))
    return pl.pallas_call(
        functools.partial(_gdn_proj_kernel, tm=tm, n_tiles=nt),
        grid=(bsz, nt),
        in_specs=_halo_specs(tm, seq) + [
            _const_spec((1, D_MODEL)),
            _const_spec((D_MODEL, 3 * A_QK)),
            _const_spec((D_MODEL, A_V)),
            _const_spec((D_MODEL, GATE_LANES)),
            _const_spec((3, 3 * A_QK)),
            _const_spec((1, GATE_LANES)),
            _const_spec((1, GATE_LANES)),
        ],
        out_specs=[act_spec(A_QK), act_spec(A_QK), act_spec(A_V), act_spec(A_V),
                   pl.BlockSpec((1, tm, GATE_LANES), lambda b, i: (b, i, 0)),
                   pl.BlockSpec((1, 4 * A_HEADS, tm), lambda b, i: (b, 0, i))],
        out_shape=[act(A_QK), act(A_QK), act(A_V), act(A_V),
                   jax.ShapeDtypeStruct((bsz, seq, GATE_LANES), F32),
                   jax.ShapeDtypeStruct((bsz, 4 * A_HEADS, seq), F32)],
        compiler_params=_params("parallel", "parallel"),
        name="gdn_project",
    )(x, x, x, g_pre, wqkv, wz, wab, conv_w, alog_row, dtb_row)


GDN_BLK = 256


def _gdn_direction(q_ref, k_ref, v_ref, gcol_ref, grow_ref, s_ref, o_ref, head, reverse):
    n = GDN_BLK
    q = q_ref[0].astype(F32)
    k = k_ref[0].astype(F32)
    v = v_ref[0].astype(F32)
    d = 1 if reverse else 0
    gcol = gcol_ref[0]
    lane = lax.broadcasted_iota(jnp.int32, (1, GATE_LANES), 1)
    pick = lambda c: jnp.sum(jnp.where(lane == c, gcol, 0.0), axis=-1, keepdims=True)
    gc = pick(d * 16 + head)
    beta = pick(d * 16 + 8 + head)
    gr = grow_ref[0, pl.ds(d * 16 + head, 1), :]

    ri = lax.broadcasted_iota(jnp.int32, (n, n), 0)
    ci = lax.broadcasted_iota(jnp.int32, (n, n), 1)
    same = (ri // A_CHUNK) == (ci // A_CHUNK)
    incl = jnp.logical_and(same, (ri <= ci) if reverse else (ri >= ci))
    strict = jnp.logical_and(incl, ri != ci)
    decay = jnp.where(incl, jnp.exp(jnp.where(incl, gc - gr, 0.0)), 0.0)

    kb = k * beta
    kb16 = kb.astype(BF16)
    k16 = k.astype(BF16)
    a = jnp.where(strict, _dot_nt(kb16, k16) * decay, 0.0)
    attn = _dot_nt(q.astype(BF16), k16) * decay

    eye = jnp.where(ri == ci, 1.0, 0.0)
    t_inv = eye - a
    a_pow = a
    s = 2
    while s < A_CHUNK:
        ap16 = a_pow.astype(BF16)
        a_pow = _dot(ap16, ap16)
        t_inv = t_inv + _dot(t_inv.astype(BF16), a_pow.astype(BF16))
        s *= 2

    eg = jnp.exp(gc)
    uw = _dot(t_inv.astype(BF16), jnp.concatenate([v * beta, kb * eg], axis=1).astype(BF16))
    u = uw[:, :A_DV]
    w = uw[:, A_DV:]
    qd = q * eg
    attn16 = attn.astype(BF16)

    state = s_ref[...]
    n_chunks = n // A_CHUNK
    order = range(n_chunks - 1, -1, -1) if reverse else range(n_chunks)
    for c in order:
        r0 = c * A_CHUNK
        rows = slice(r0, r0 + A_CHUNK)
        last = r0 if reverse else r0 + A_CHUNK - 1
        g_last = gc[last:last + 1, :]
        s16 = state.astype(BF16)
        wq = _dot(jnp.concatenate([w[rows], qd[rows]], axis=0).astype(BF16), s16)
        v_new = u[rows] - wq[:A_CHUNK]
        vn16 = v_new.astype(BF16)
        o_ref[0, rows, :] = wq[A_CHUNK:] + _dot(attn16[rows, r0:r0 + A_CHUNK], vn16)
        kd = k[rows] * jnp.exp(g_last - gc[rows])
        state = state * jnp.exp(g_last) + _dot_tn(kd.astype(BF16), vn16)
    s_ref[...] = state


def _gdn_core_kernel(qf_ref, kf_ref, vf_ref, gcf_ref, grf_ref, qb_ref, kb_ref, vb_ref, gcb_ref, grb_ref,
                     of_ref, ob_ref, sf_ref, sb_ref):
    head = pl.program_id(1)

    @pl.when(pl.program_id(2) == 0)
    def _():
        sf_ref[...] = jnp.zeros_like(sf_ref)
        sb_ref[...] = jnp.zeros_like(sb_ref)

    _gdn_direction(qf_ref, kf_ref, vf_ref, gcf_ref, grf_ref, sf_ref, of_ref, head, False)
    _gdn_direction(qb_ref, kb_ref, vb_ref, gcb_ref, grb_ref, sb_ref, ob_ref, head, True)


def gdn_core(q, k, v, gcol, grow):
    bsz, seq, _ = q.shape
    n = GDN_BLK
    nb = seq // n
    fwd = lambda b, h, j: (b, j, h)
    bwd = lambda b, h, j: (b, nb - 1 - j, h)
    specs = []
    for im in (fwd, bwd):
        specs += [pl.BlockSpec((1, n, A_DK), im), pl.BlockSpec((1, n, A_DK), im), pl.BlockSpec((1, n, A_DV), im),
                  pl.BlockSpec((1, n, GATE_LANES), (lambda im: lambda b, h, j: (b, im(b, h, j)[1], 0))(im)),
                  pl.BlockSpec((1, 4 * A_HEADS, n), (lambda im: lambda b, h, j: (b, 0, im(b, h, j)[1]))(im))]
    out = jax.ShapeDtypeStruct((bsz, seq, A_V), F32)
    return pl.pallas_call(
        _gdn_core_kernel,
        grid=(bsz, A_HEADS, nb),
        in_specs=specs,
        out_specs=[pl.BlockSpec((1, n, A_DV), fwd), pl.BlockSpec((1, n, A_DV), bwd)],
        out_shape=[out, out],
        scratch_shapes=[pltpu.VMEM((A_DK, A_DV), F32), pltpu.VMEM((A_DK, A_DV), F32)],
        compiler_params=_params("parallel", "parallel", "arbitrary"),
        name="gdn_core",
    )(q, k, v, gcol, grow, q, k, v, gcol, grow)


OUT_TM = 512


def _gdn_out_kernel(of_ref, ob_ref, z_ref, x_ref, nw_ref, wo_ref, g_ref, y_ref):
    nw = nw_ref[...]
    parts = []
    for h in range(A_HEADS):
        cols = slice(h * A_DV, (h + 1) * A_DV)
        o = of_ref[:, cols] + ob_ref[:, cols]
        o = o * lax.rsqrt(jnp.mean(o * o, axis=-1, keepdims=True) + EPS) * nw
        parts.append((o * _silu(z_ref[:, cols].astype(F32))).astype(BF16))
    hout = _dot(jnp.concatenate(parts, axis=1), wo_ref[...])
    y_ref[...] = x_ref[...] + _rms(hout, g_ref[...])


def gdn_output(o_f, o_b, z, x2, norm_w, w_out, g_post):
    t = x2.shape[0]
    tm = OUT_TM
    row = lambda c: pl.BlockSpec((tm, c), lambda i: (i, 0))
    return pl.pallas_call(
        _gdn_out_kernel,
        grid=(t // tm,),
        in_specs=[row(A_V), row(A_V), row(A_V), row(D_MODEL), _const_spec((1, A_DV)),
                  _const_spec((A_V, D_MODEL)), _const_spec((1, D_MODEL))],
        out_specs=row(D_MODEL),
        out_shape=jax.ShapeDtypeStruct((t, D_MODEL), F32),
        compiler_params=_params("parallel"),
        name="gdn_output",
    )(o_f, o_b, z, x2, norm_w, w_out, g_post)


def gdn_mixer_block(x, g_pre, g_post, w_in, conv_w, a_log, dt_bias, norm_w, w_out):
    bsz, seq, _ = x.shape
    n_qkv = 2 * A_QK + A_V
    wqkv = w_in[:, :n_qkv].astype(BF16)
    wz = w_in[:, n_qkv:n_qkv + A_V].astype(BF16)
    pad = GATE_LANES - 4 * A_HEADS
    wab = jnp.pad(w_in[:, n_qkv + A_V:], ((0, 0), (0, pad))).astype(BF16)
    zeros = jnp.zeros((2, A_HEADS), F32)
    alog_row = jnp.pad(jnp.stack([a_log, zeros], axis=1).reshape(1, 4 * A_HEADS), ((0, 0), (0, pad)))
    dtb_row = jnp.pad(jnp.stack([dt_bias, zeros], axis=1).reshape(1, 4 * A_HEADS), ((0, 0), (0, pad)))
    q, k, v, z, gcol, grow = gdn_project(x, g_pre.reshape(1, -1), wqkv, wz, wab, conv_w, alog_row, dtb_row)
    o_f, o_b = gdn_core(q, k, v, gcol, grow)
    t = bsz * seq
    y = gdn_output(o_f.reshape(t, A_V), o_b.reshape(t, A_V), z.reshape(t, A_V), x.reshape(t, D_MODEL),
                   norm_w.reshape(1, A_DV), w_out.astype(BF16), g_post.reshape(1, -1))
    return y.reshape(bsz, seq, D_MODEL)


RET_BLK = 256
DECAY_ROWS = SUBLANE


def _ret_proj_kernel(x_ref, g_ref, w_ref, cos_ref, sin_ref, q_ref, k_ref, v_ref, gate_ref):
    xn = _rms(x_ref[0], g_ref[...]).astype(BF16)
    cos = cos_ref[...]
    sin = sin_ref[...]
    nq = B_HEADS * B_DK
    nv = B_HEADS * B_DV
    half = B_DK // 2
    for part, (ref, scale) in enumerate(((q_ref, 1.0), (k_ref, B_DK ** -0.5))):
        p = _dot(xn, w_ref[:, part * nq:(part + 1) * nq])
        for h in range(B_HEADS):
            x1 = p[:, h * B_DK:h * B_DK + half]
            x2 = p[:, h * B_DK + half:(h + 1) * B_DK]
            ref[0, :, h * B_DK:h * B_DK + half] = ((x1 * cos - x2 * sin) * scale).astype(BF16)
            ref[0, :, h * B_DK + half:(h + 1) * B_DK] = ((x2 * cos + x1 * sin) * scale).astype(BF16)
    v_ref[0] = _dot(xn, w_ref[:, 2 * nq:2 * nq + nv]).astype(BF16)
    gate_ref[0] = _dot(xn, w_ref[:, 2 * nq + nv:]).astype(BF16)


def ret_project(x, g_pre, w_in, cos, sin):
    bsz, seq, _ = x.shape
    tm = PROJ_TM
    nq = B_HEADS * B_DK
    nv = B_HEADS * B_DV
    act = lambda c: jax.ShapeDtypeStruct((bsz, seq, c), BF16)
    act_spec = lambda c: pl.BlockSpec((1, tm, c), lambda b, i: (b, i, 0))
    tab_spec = pl.BlockSpec((tm, B_DK // 2), lambda b, i: (i, 0))
    return pl.pallas_call(
        _ret_proj_kernel,
        grid=(bsz, seq // tm),
        in_specs=[act_spec(D_MODEL), _const_spec((1, D_MODEL)), _const_spec((D_MODEL, 2 * nq + 2 * nv)),
                  tab_spec, tab_spec],
        out_specs=[act_spec(nq), act_spec(nq), act_spec(nv), act_spec(nv)],
        out_shape=[act(nq), act(nq), act(nv), act(nv)],
        compiler_params=_params("parallel", "parallel"),
        name="ret_project",
    )(x, g_pre, w_in, cos, sin)


def _ret_core_kernel(dl_ref, qf_ref, kf_ref, vf_ref, qb_ref, kb_ref, vb_ref, of_ref, ob_ref, rf_ref, rb_ref):
    head = pl.program_id(1)
    n = RET_BLK

    @pl.when(pl.program_id(2) == 0)
    def _():
        rf_ref[...] = jnp.zeros_like(rf_ref)
        rb_ref[...] = jnp.zeros_like(rb_ref)

    log_gamma = jnp.log1p(-jnp.exp2(dl_ref[...]))
    lane = lax.broadcasted_iota(jnp.int32, (1, LANE), 1)
    lg = jnp.sum(jnp.where(lane == head, log_gamma, 0.0), axis=-1, keepdims=True)
    lgf = lg[0:1]
    lgb = lg[1:2]

    ri = lax.broadcasted_iota(jnp.int32, (n, n), 0)
    ci = lax.broadcasted_iota(jnp.int32, (n, n), 1)
    dist = (ri - ci).astype(F32)
    mask = (jnp.where(dist >= 0, jnp.exp(lgf * jnp.maximum(dist, 0.0)), 0.0)
            + jnp.where(dist <= 0, jnp.exp(lgb * jnp.maximum(-dist, 0.0)), 0.0))
    pos = lax.broadcasted_iota(jnp.int32, (n, 1), 0).astype(F32)

    q = qf_ref[0]
    k = kf_ref[0]
    v = vf_ref[0]
    inner = _dot((_dot_nt(q, k) * mask).astype(BF16), v)
    rf = rf_ref[...]
    cross = _dot((q.astype(F32) * jnp.exp(lgf * (pos + 1.0))).astype(BF16), rf.astype(BF16))
    of_ref[0] = inner + cross
    kz = (k.astype(F32) * jnp.exp(lgf * (n - 1.0 - pos))).astype(BF16)
    rf_ref[...] = rf * jnp.exp(lgf * n) + _dot_tn(kz, v)

    q = qb_ref[0]
    k = kb_ref[0]
    v = vb_ref[0]
    rb = rb_ref[...]
    ob_ref[0] = _dot((q.astype(F32) * jnp.exp(lgb * (n - pos))).astype(BF16), rb.astype(BF16))
    kz = (k.astype(F32) * jnp.exp(lgb * pos)).astype(BF16)
    rb_ref[...] = rb * jnp.exp(lgb * n) + _dot_tn(kz, v)


def ret_core(decay_logit_tile, q, k, v):
    bsz, seq, _ = q.shape
    n = RET_BLK
    nb = seq // n
    fwd = lambda b, h, j: (b, j, h)
    bwd = lambda b, h, j: (b, nb - 1 - j, h)
    specs = [_const_spec((DECAY_ROWS, LANE))]
    for im in (fwd, bwd):
        specs += [pl.BlockSpec((1, n, B_DK), im), pl.BlockSpec((1, n, B_DK), im), pl.BlockSpec((1, n, B_DV), im)]
    out = jax.ShapeDtypeStruct((bsz, seq, B_HEADS * B_DV), F32)
    return pl.pallas_call(
        _ret_core_kernel,
        grid=(bsz, B_HEADS, nb),
        in_specs=specs,
        out_specs=[pl.BlockSpec((1, n, B_DV), fwd), pl.BlockSpec((1, n, B_DV), bwd)],
        out_shape=[out, out],
        scratch_shapes=[pltpu.VMEM((B_DK, B_DV), F32), pltpu.VMEM((B_DK, B_DV), F32)],
        compiler_params=_params("parallel", "parallel", "arbitrary"),
        name="ret_core",
    )(decay_logit_tile, q, k, v, q, k, v)


def _ret_out_kernel(of_ref, ob_ref, gate_ref, x_ref, gn_ref, wo_ref, g_ref, y_ref):
    parts = []
    for h in range(B_HEADS):
        cols = slice(h * B_DV, (h + 1) * B_DV)
        o = of_ref[:, cols] + ob_ref[:, cols]
        oc = o - jnp.mean(o, axis=-1, keepdims=True)
        on = oc * lax.rsqrt(jnp.mean(oc * oc, axis=-1, keepdims=True) + EPS) * gn_ref[:, cols]
        parts.append((_silu(gate_ref[:, cols].astype(F32)) * on).astype(BF16))
    hout = _dot(jnp.concatenate(parts, axis=1), wo_ref[...])
    y_ref[...] = x_ref[...] + _rms(hout, g_ref[...])


def ret_output(o_f, o_b, gate, x2, gn_w, w_out, g_post):
    t = x2.shape[0]
    tm = OUT_TM
    nv = B_HEADS * B_DV
    row = lambda c: pl.BlockSpec((tm, c), lambda i: (i, 0))
    return pl.pallas_call(
        _ret_out_kernel,
        grid=(t // tm,),
        in_specs=[row(nv), row(nv), row(nv), row(D_MODEL), _const_spec((1, nv)),
                  _const_spec((nv, D_MODEL)), _const_spec((1, D_MODEL))],
        out_specs=row(D_MODEL),
        out_shape=jax.ShapeDtypeStruct((t, D_MODEL), F32),
        compiler_params=_params("parallel"),
        name="ret_output",
    )(o_f, o_b, gate, x2, gn_w, w_out, g_post)


def _rope_tables(seq):
    inv = ROPE_BASE ** (-np.arange(0, B_DK, 2, dtype=np.float64) / B_DK)
    ang = np.arange(seq, dtype=np.float64)[:, None] * inv[None, :]
    return jnp.asarray(np.cos(ang), F32), jnp.asarray(np.sin(ang), F32)


def retnet_mixer_block(x, g_pre, g_post, w_in, decay_logit, gn_w, w_out):
    bsz, seq, _ = x.shape
    cos, sin = _rope_tables(seq)
    q, k, v, gate = ret_project(x, g_pre.reshape(1, -1), w_in.astype(BF16), cos, sin)
    dl = jnp.pad(decay_logit, ((0, DECAY_ROWS - 2), (0, LANE - B_HEADS)), constant_values=-1.0)
    o_f, o_b = ret_core(dl, q, k, v)
    t = bsz * seq
    nv = B_HEADS * B_DV
    y = ret_output(o_f.reshape(t, nv), o_b.reshape(t, nv), gate.reshape(t, nv), x.reshape(t, D_MODEL),
                   gn_w.reshape(1, nv), w_out.astype(BF16), g_post.reshape(1, -1))
    return y.reshape(bsz, seq, D_MODEL)


DFT_N2 = 128
HY_LANES = D_MODEL
HY_CB = 512
FILTER_TL = 256


def _dft_tables(seq):
    n = 2 * seq
    n2 = DFT_N2
    n1 = n // n2
    kk = np.arange(n1, dtype=np.float64)[:, None]
    nn = np.arange(n1 // 2, dtype=np.float64)[None, :]
    ang = 2.0 * np.pi * kk * nn / n1
    cc, ss = np.cos(ang), np.sin(ang)
    m1 = np.block([[cc, ss], [-ss, cc]])
    m1_real = np.concatenate([cc, -ss], axis=0)
    m3 = m1.T / n
    k1 = np.arange(n1, dtype=np.float64)[:, None, None]
    k2 = np.arange(n2, dtype=np.float64)[None, :, None]
    t2 = np.arange(n2, dtype=np.float64)[None, None, :]
    theta = 2.0 * np.pi * (t2 * k2 / n2 + t2 * k1 / n)
    c, s = np.cos(theta), np.sin(theta)
    m2f = np.concatenate([np.concatenate([c, s], axis=2), np.concatenate([-s, c], axis=2)], axis=1)
    m2i = np.transpose(m2f, (0, 2, 1))
    cast = lambda a: jnp.asarray(a, F32).astype(BF16)
    return n1, cast(m1), cast(m1_real), cast(m3), cast(m2f), cast(m2i)


def _hy_proj_kernel(xm_ref, xp_ref, xn_ref, g_ref, w_ref, bin_ref, cw_ref, cb_ref,
                    v_ref, g1_ref, g2_ref, *, tm, n_tiles):
    xne = _normed_with_halo(xm_ref, xp_ref, xn_ref, g_ref[...])
    valid = _halo_valid(tm, n_tiles)
    for part, ref in enumerate((v_ref, g1_ref, g2_ref)):
        cols = slice(part * D_MODEL, (part + 1) * D_MODEL)
        p = jnp.where(valid, _dot(xne, w_ref[:, cols]) + bin_ref[:, cols], 0.0)
        ref[0] = _conv3(p, cw_ref[:, cols], tm) + cb_ref[:, cols]


def hy_project(x, g_pre, w_in, b_in, conv_w, conv_b):
    bsz, seq, _ = x.shape
    tm = PROJ_TM
    nt = seq // tm
    out = jax.ShapeDtypeStruct((bsz, seq, D_MODEL), F32)
    spec = pl.BlockSpec((1, tm, D_MODEL), lambda b, i: (b, i, 0))
    return pl.pallas_call(
        functools.partial(_hy_proj_kernel, tm=tm, n_tiles=nt),
        grid=(bsz, nt),
        in_specs=_halo_specs(tm, seq) + [
            _const_spec((1, D_MODEL)), _const_spec((D_MODEL, 3 * D_MODEL)), _const_spec((1, 3 * D_MODEL)),
            _const_spec((3, 3 * D_MODEL)), _const_spec((1, 3 * D_MODEL))],
        out_specs=[spec, spec, spec],
        out_shape=[out, out, out],
        compiler_params=_params("parallel", "parallel"),
        name="hy_project",
    )(x, x, x, g_pre, w_in, b_in, conv_w, conv_b)


def _dot_f32(a, b):
    return jnp.dot(a, b, preferred_element_type=F32, precision=lax.Precision.HIGHEST)


def _hy_filter_kernel(f_ref, w1t_ref, w1c_ref, w1s_ref, b1_ref, fr1_ref, w2_ref, b2_ref, fr2_ref, w3_ref,
                      delta_ref, h_ref, *, tl, seq):
    n = (lax.broadcasted_iota(jnp.int32, (tl, 1), 0) + pl.program_id(0) * tl).astype(F32)
    t = n / (seq - 1.0)
    arg = (2.0 * math.pi / seq) * n * f_ref[...]
    pre = (t * w1t_ref[...] + _dot_f32(jnp.cos(arg), w1c_ref[...]) + _dot_f32(-jnp.sin(arg), w1s_ref[...])
           + b1_ref[...])
    h1 = jnp.sin(fr1_ref[...] * pre)
    h2 = jnp.sin(fr2_ref[...] * (_dot_f32(h1, w2_ref[...]) + b2_ref[...]))
    h3 = _dot_f32(h2, w3_ref[...])
    window = jnp.exp(-t * delta_ref[...])
    for s in range(4):
        hs = h3[:, s * D_MODEL:(s + 1) * D_MODEL] * window
        if s >= 2:
            hs = jnp.where(n == 0.0, 0.0, hs)
        h_ref[s] = hs


def hy_filters(seq, f_w1, f_b1, f_fr1, f_w2, f_b2, f_fr2, f_w3):
    bands = (C_EMB - 1) // 2
    fw = C_FILTER_WIDTH
    f = np.zeros((1, LANE), np.float32)
    f[0, :bands] = np.linspace(1e-4, bands - 1, bands, dtype=np.float32)
    deltas = np.abs(np.linspace(C_MIN_DECAY, C_MAX_DECAY, D_MODEL, dtype=np.float32))[None, :]
    pad = ((0, LANE - bands), (0, 0))
    w1c = jnp.pad(f_w1[1:1 + bands], pad)
    w1s = jnp.pad(f_w1[1 + bands:], pad)
    tl = FILTER_TL
    row = lambda a: a.reshape(1, -1)
    args = (jnp.asarray(f), f_w1[0:1], w1c, w1s, row(f_b1), row(f_fr1), f_w2, row(f_b2), row(f_fr2), f_w3,
            jnp.asarray(deltas))
    return pl.pallas_call(
        functools.partial(_hy_filter_kernel, tl=tl, seq=seq),
        grid=(seq // tl,),
        in_specs=[_const_spec(a.shape) for a in args],
        out_specs=pl.BlockSpec((4, tl, D_MODEL), lambda i: (0, i, 0)),
        out_shape=jax.ShapeDtypeStruct((4, seq, D_MODEL), F32),
        compiler_params=_params("parallel"),
        name="hy_filters",
    )(*args)


def _dft1_kernel(u_ref, m1_ref, y_ref):
    p, half, lw = u_ref.shape[1:]
    y_ref[0] = _dot(m1_ref[...], u_ref[0].reshape(p * half, lw).astype(BF16)).astype(BF16)


def dft_stage1(u4, m1):
    g, p, half, width = u4.shape
    rows = m1.shape[0]
    lw = HY_LANES
    return pl.pallas_call(
        _dft1_kernel,
        grid=(g, width // lw),
        in_specs=[pl.BlockSpec((1, p, half, lw), lambda b, j: (b, 0, 0, j)), _const_spec(m1.shape)],
        out_specs=pl.BlockSpec((1, rows, lw), lambda b, j: (b, 0, j)),
        out_shape=jax.ShapeDtypeStruct((g, rows, width), BF16),
        compiler_params=_params("parallel", "parallel"),
        name="dft_stage1",
    )(u4, m1)


def _slab(ref):
    n2, cb = ref.shape[3:]
    return ref[0, :, 0].reshape(2 * n2, cb)


def _filter_spec_kernel(y_ref, m2f_ref, x_ref):
    n2, cb = y_ref.shape[3:]
    x_ref[0, :, 0] = _dot(m2f_ref[0], _slab(y_ref)).reshape(2, n2, cb)


def filter_spectrum(y5, m2f):
    s, _, n1, n2, c = y5.shape
    cb = HY_CB
    blk = pl.BlockSpec((1, 2, 1, n2, cb), lambda k, j, b: (b, 0, k, 0, j))
    return pl.pallas_call(
        _filter_spec_kernel,
        grid=(n1, c // cb, s),
        in_specs=[blk, pl.BlockSpec((1, 2 * n2, 2 * n2), lambda k, j, b: (k, 0, 0))],
        out_specs=blk,
        out_shape=jax.ShapeDtypeStruct(y5.shape, F32),
        compiler_params=_params("parallel", "parallel", "parallel"),
        name="filter_spectrum",
    )(y5, m2f)


def _conv_freq_kernel(y_ref, m2f_ref, m2i_ref, hf_ref, hb_ref, z_ref):
    n2, cb = y_ref.shape[3:]
    x = _dot(m2f_ref[0], _slab(y_ref))
    xr, xi = x[:n2], x[n2:]
    hr = hf_ref[0, 0, 0] + hb_ref[0, 0, 0]
    hi = hf_ref[0, 1, 0] - hb_ref[0, 1, 0]
    prod = jnp.concatenate([xr * hr - xi * hi, xr * hi + xi * hr], axis=0).astype(BF16)
    z_ref[0, :, 0] = _dot(m2i_ref[0], prod).reshape(2, n2, cb).astype(BF16)


def conv_freq(y5, m2f, m2i, hspec, order):
    g, _, n1, n2, c = y5.shape
    cb = HY_CB
    blk = pl.BlockSpec((1, 2, 1, n2, cb), lambda k, j, b: (b, 0, k, 0, j))
    tab = pl.BlockSpec((1, 2 * n2, 2 * n2), lambda k, j, b: (k, 0, 0))
    hblk = lambda s: pl.BlockSpec((1, 2, 1, n2, cb), lambda k, j, b: (s, 0, k, 0, j))
    return pl.pallas_call(
        _conv_freq_kernel,
        grid=(n1, c // cb, g),
        in_specs=[blk, tab, tab, hblk(order), hblk(2 + order)],
        out_specs=blk,
        out_shape=jax.ShapeDtypeStruct(y5.shape, BF16),
        compiler_params=_params("parallel", "parallel", "parallel"),
        name="conv_freq",
    )(y5, m2f, m2i, hspec, hspec)


def _conv_time_kernel(z_ref, u_ref, gate_ref, skip_ref, m3_ref, *rest, chain):
    half, lw = u_ref.shape[2:]
    y = _dot(m3_ref[...], z_ref[0])
    u = u_ref[0].reshape(2 * half, lw)
    out = gate_ref[0].reshape(2 * half, lw) * (y + skip_ref[...] * u)
    if chain:
        m1_ref, o_ref, y_ref = rest
        y_ref[0] = _dot(m1_ref[...], out.astype(BF16)).astype(BF16)
    else:
        (o_ref,) = rest
    o_ref[0] = out.reshape(2, half, lw)


def conv_time(z3, u4, gate4, skip, m3, m1=None):
    g, _, half, width = u4.shape
    rows = z3.shape[1]
    lw = HY_LANES
    chain = m1 is not None
    sig = pl.BlockSpec((1, 2, half, lw), lambda b, j: (b, 0, 0, j))
    spec = pl.BlockSpec((1, rows, lw), lambda b, j: (b, 0, j))
    in_specs = [spec, sig, sig, _const_spec((1, lw)), _const_spec(m3.shape)]
    out_specs = [sig]
    out_shape = [jax.ShapeDtypeStruct(u4.shape, F32)]
    args = [z3, u4, gate4, skip, m3]
    if chain:
        in_specs.append(_const_spec(m1.shape))
        out_specs.append(spec)
        out_shape.append(jax.ShapeDtypeStruct(z3.shape, BF16))
        args.append(m1)
    res = pl.pallas_call(
        functools.partial(_conv_time_kernel, chain=chain),
        grid=(g, width // lw),
        in_specs=in_specs,
        out_specs=out_specs,
        out_shape=out_shape,
        compiler_params=_params("parallel", "parallel"),
        name="conv_time_chain" if chain else "conv_time",
    )(*args)
    return res if chain else res[0]


def _hy_out_kernel(z_ref, x_ref, wo_ref, bo_ref, g_ref, y_ref):
    hout = _dot(z_ref[...].astype(BF16), wo_ref[...]) + bo_ref[...]
    y_ref[...] = x_ref[...] + _rms(hout, g_ref[...])


def hy_output(z2, x2, w_out, b_out, g_post):
    t = x2.shape[0]
    tm = OUT_TM
    row = pl.BlockSpec((tm, D_MODEL), lambda i: (i, 0))
    return pl.pallas_call(
        _hy_out_kernel,
        grid=(t // tm,),
        in_specs=[row, row, _const_spec((D_MODEL, D_MODEL)), _const_spec((1, D_MODEL)), _const_spec((1, D_MODEL))],
        out_specs=row,
        out_shape=jax.ShapeDtypeStruct((t, D_MODEL), F32),
        compiler_params=_params("parallel"),
        name="hy_output",
    )(z2, x2, w_out, b_out, g_post)


def hyena_mixer_block(x, g_pre, g_post, w_in, b_in, conv_w, conv_b, f_w1, f_b1, f_fr1, f_w2, f_b2, f_fr2, f_w3,
                      bias_d, w_out, b_out):
    bsz, seq, _ = x.shape
    n1, m1, m1_real, m3, m2f, m2i = _dft_tables(seq)
    n2 = DFT_N2
    width = n2 * D_MODEL
    pairs = bsz // 2
    sig4 = lambda a: a.reshape(pairs, 2, n1 // 2, width)
    five = lambda a: a.reshape(a.shape[0], 2, n1, n2, D_MODEL)

    filt = hy_filters(seq, f_w1, f_b1, f_fr1, f_w2, f_b2, f_fr2, f_w3)
    hspec = filter_spectrum(five(dft_stage1(filt.reshape(4, 1, n1 // 2, width), m1_real)), m2f)

    v, g1, g2 = hy_project(x, g_pre.reshape(1, -1), w_in.astype(BF16), b_in.reshape(1, -1), conv_w,
                           conv_b.reshape(1, -1))
    v4, g14, g24 = sig4(v), sig4(g1), sig4(g2)
    y = dft_stage1(v4, m1)
    z = conv_freq(five(y), m2f, m2i, hspec, 0).reshape(pairs, 2 * n1, width)
    z1, y = conv_time(z, v4, g14, bias_d[0:1], m3, m1)
    z = conv_freq(five(y), m2f, m2i, hspec, 1).reshape(pairs, 2 * n1, width)
    z2 = conv_time(z, z1, g24, bias_d[1:2], m3)
    t = bsz * seq
    out = hy_output(z2.reshape(t, D_MODEL), x.reshape(t, D_MODEL), w_out.astype(BF16), b_out.reshape(1, -1),
                    g_post.reshape(1, -1))
    return out.reshape(bsz, seq, D_MODEL)


def kernel(x_prompt, x_sample, norm_g, ffn_w_gate, ffn_w_up, ffn_w_down, a_w_in, a_conv_w, a_a_log, a_dt_bias,
           a_norm_w, a_w_out, b_w_in, b_decay_logit, b_gn_w, b_w_out, c_w_in, c_b_in, c_conv_w, c_conv_b,
           c_f_w1, c_f_b1, c_f_freq1, c_f_w2, c_f_b2, c_f_freq2, c_f_w3, c_bias_d, c_w_out, c_b_out):
    depth = norm_g.shape[0]
    wg = ffn_w_gate.astype(BF16)
    wu = ffn_w_up.astype(BF16)
    wd = ffn_w_down.astype(BF16)

    def ffn(x, i, which):
        bsz, seq, _ = x.shape
        g = norm_g[i, 4 * which:4 * which + 2]
        return ffn_block(x.reshape(bsz * seq, D_MODEL), g, wg[i, which], wu[i, which],
                         wd[i, which]).reshape(bsz, seq, D_MODEL)

    def mixer(x, i):
        kind, j = i % 3, i // 3
        g_pre, g_post = norm_g[i, 2], norm_g[i, 3]
        if kind == 0:
            return gdn_mixer_block(x, g_pre, g_post, a_w_in[j], a_conv_w[j], a_a_log[j], a_dt_bias[j],
                                   a_norm_w[j], a_w_out[j])
        if kind == 1:
            return retnet_mixer_block(x, g_pre, g_post, b_w_in[j], b_decay_logit[j], b_gn_w[j], b_w_out[j])
        return hyena_mixer_block(x, g_pre, g_post, c_w_in[j], c_b_in[j], c_conv_w[j], c_conv_b[j], c_f_w1[j],
                                 c_f_b1[j], c_f_freq1[j], c_f_w2[j], c_f_b2[j], c_f_freq2[j], c_f_w3[j],
                                 c_bias_d[j], c_w_out[j], c_b_out[j])

    def trunk(x):
        for i in range(depth):
            x = ffn(x, i, 0)
            x = mixer(x, i)
            x = ffn(x, i, 1)
        return x

    return (trunk(x_prompt), trunk(x_sample))
```

```python
import functools
import math

import numpy as np
import jax
import jax.numpy as jnp
from jax import lax
from jax.experimental import pallas as pl
from jax.experimental.pallas import tpu as pltpu

F32 = jnp.float32
BF16 = jnp.bfloat16
EPS = 1e-6

D_MODEL = 1024
D_FF = 2816

A_HEADS, A_DK, A_DV, A_CHUNK = 8, 128, 128, 64
A_QK = A_HEADS * A_DK
A_V = A_HEADS * A_DV
B_HEADS, B_DK, B_DV = 4, 256, 512
ROPE_BASE = 10000.0
C_EMB, C_FILTER_WIDTH = 33, 64
C_TARGET, C_SHORT_DECAY_PCT, C_LONG_DECAY_PCT = 1e-2, 0.3, 1.5
C_MIN_DECAY = math.log(C_TARGET) / C_LONG_DECAY_PCT
C_MAX_DECAY = math.log(C_TARGET) / C_SHORT_DECAY_PCT

V7X_VMEM_BYTES = 64 * 1024 * 1024
VMEM_LIMIT = V7X_VMEM_BYTES - 8 * 1024 * 1024
SUBLANE = 8
LANE = 128
MXU_DIM = 256

HALO = SUBLANE


def _params(*sem):
    return pltpu.CompilerParams(dimension_semantics=sem, vmem_limit_bytes=VMEM_LIMIT)


def _const_spec(shape):
    nd = len(shape)
    return pl.BlockSpec(shape, lambda *_: (0,) * nd, pipeline_mode=pl.Buffered(1))


def _rms(x, g):
    return x * lax.rsqrt(jnp.mean(x * x, axis=-1, keepdims=True) + EPS) * g


def _silu(x):
    return x * jax.nn.sigmoid(x)


def _dot(a, b):
    return jnp.dot(a, b, preferred_element_type=F32)


def _dot_nt(a, b):
    return lax.dot_general(a, b, (((1,), (1,)), ((), ())), preferred_element_type=F32)


def _dot_tn(a, b):
    return lax.dot_general(a, b, (((0,), (0,)), ((), ())), preferred_element_type=F32)


FFN_TM = 512
FFN_CHUNKS = 2


def _ffn_kernel(x_ref, g_ref, wg_ref, wu_ref, wd_ref, o_ref):
    x = x_ref[...]
    g = g_ref[...]
    xn = _rms(x, g[0:1]).astype(BF16)
    fc = D_FF // FFN_CHUNKS
    y = None
    for c in range(FFN_CHUNKS):
        hg = _dot(xn, wg_ref[:, c * fc:(c + 1) * fc])
        hu = _dot(xn, wu_ref[:, c * fc:(c + 1) * fc])
        act = (_silu(hg) * hu).astype(BF16)
        part = _dot(act, wd_ref[c * fc:(c + 1) * fc, :])
        y = part if y is None else y + part
    o_ref[...] = x + 0.5 * _rms(y, g[1:2])


def ffn_block(x2, g_pair, wg, wu, wd):
    t = x2.shape[0]
    tm = FFN_TM
    return pl.pallas_call(
        _ffn_kernel,
        grid=(t // tm,),
        in_specs=[
            pl.BlockSpec((tm, D_MODEL), lambda i: (i, 0)),
            _const_spec((2, D_MODEL)),
            _const_spec((D_MODEL, D_FF)),
            _const_spec((D_MODEL, D_FF)),
            _const_spec((D_FF, D_MODEL)),
        ],
        out_specs=pl.BlockSpec((tm, D_MODEL), lambda i: (i, 0)),
        out_shape=jax.ShapeDtypeStruct((t, D_MODEL), F32),
        compiler_params=_params("parallel"),
        name="ffn_block",
    )(x2, g_pair, wg, wu, wd)


PROJ_TM = 512


def _halo_specs(tm, seq):
    per = tm // HALO
    last = seq // HALO - 1
    return [
        pl.BlockSpec((1, tm, D_MODEL), lambda b, i: (b, i, 0)),
        pl.BlockSpec((1, HALO, D_MODEL), lambda b, i: (b, jnp.maximum(i * per - 1, 0), 0)),
        pl.BlockSpec((1, HALO, D_MODEL), lambda b, i: (b, jnp.minimum((i + 1) * per, last), 0)),
    ]


def _normed_with_halo(xm_ref, xp_ref, xn_ref, g):
    xe = jnp.concatenate([xp_ref[0], xm_ref[0], xn_ref[0]], axis=0)
    return _rms(xe, g).astype(BF16)


def _halo_valid(tm, n_tiles):
    i = pl.program_id(1)
    row = lax.broadcasted_iota(jnp.int32, (tm + 2 * HALO, 1), 0)
    before = jnp.logical_and(row < HALO, i == 0)
    after = jnp.logical_and(row >= tm + HALO, i == n_tiles - 1)
    return jnp.logical_not(jnp.logical_or(before, after))


def _conv3(p, cw, tm):
    return (p[HALO - 1:HALO - 1 + tm] * cw[0:1] + p[HALO:HALO + tm] * cw[1:2]
            + p[HALO + 1:HALO + 1 + tm] * cw[2:3])


GATE_LANES = LANE


def _chunk_scan(val, pos, size, axis, reverse):
    s = 1
    while s < A_CHUNK:
        if reverse:
            shifted = pltpu.roll(val, size - s, axis)
            val = val + jnp.where(pos < A_CHUNK - s, shifted, 0.0)
        else:
            shifted = pltpu.roll(val, s, axis)
            val = val + jnp.where(pos >= s, shifted, 0.0)
        s *= 2
    return val


def _gdn_proj_kernel(xm_ref, xp_ref, xn_ref, g_ref, wqkv_ref, wz_ref, wab_ref, cw_ref, alog_ref, dtb_ref,
                     q_ref, k_ref, v_ref, z_ref, gcol_ref, grow_ref, *, tm, n_tiles):
    xne = _normed_with_halo(xm_ref, xp_ref, xn_ref, g_ref[...])
    valid = _halo_valid(tm, n_tiles)
    outs = (q_ref, k_ref, v_ref)
    for part in range(3):
        cols = slice(part * A_QK, (part + 1) * A_QK)
        p = jnp.where(valid, _dot(xne, wqkv_ref[:, cols]), 0.0)
        a = _silu(_conv3(p, cw_ref[:, cols], tm))
        if part == 2:
            v_ref[0] = a.astype(BF16)
            continue
        scale = A_DK ** -0.5 if part == 0 else 1.0
        for h in range(A_HEADS):
            ah = a[:, h * A_DK:(h + 1) * A_DK]
            inv = lax.rsqrt(jnp.sum(ah * ah, axis=-1, keepdims=True) + EPS) * scale
            outs[part][0, :, h * A_DK:(h + 1) * A_DK] = (ah * inv).astype(BF16)
    xn_main = xne[HALO:HALO + tm]
    z_ref[0] = _dot(xn_main, wz_ref[...]).astype(BF16)
    ab = _dot(xn_main, wab_ref[...])
    col = lax.broadcasted_iota(jnp.int32, (1, GATE_LANES), 1)
    is_decay = jnp.logical_and(col % 16 < 8, col < 32)
    sp = ab + dtb_ref[...]
    softplus = jnp.maximum(sp, 0.0) + jnp.log(1.0 + jnp.exp(-jnp.abs(sp)))
    val = jnp.where(is_decay, -jnp.exp(alog_ref[...]) * softplus, jax.nn.sigmoid(ab))
    pos = lax.broadcasted_iota(jnp.int32, (tm, 1), 0) % A_CHUNK
    fwd = _chunk_scan(val, pos, tm, 0, False)
    bwd = _chunk_scan(val, pos, tm, 0, True)
    gate = jnp.where(col < 8, fwd, jnp.where(jnp.logical_and(col >= 16, col < 24), bwd, val))
    gcol_ref[0] = gate
    grow_ref[0] = gate.T[0:4 * A_HEADS, :]


def gdn_project(x, g_pre, wqkv, wz, wab, conv_w, alog_row, dtb_row):
    bsz, seq, _ = x.shape
    tm = PROJ_TM
    nt = seq // tm
    act = lambda c: jax.ShapeDtypeStruct((bsz, seq, c), BF16)
    act_spec = lambda c: pl.BlockSpec((1, tm, c), lambda b, i: (b, i, 0))
    return pl.pallas_call(
        functools.partial(_gdn_proj_kernel, tm=tm, n_tiles=nt),
        grid=(bsz, nt),
        in_specs=_halo_specs(tm, seq) + [
            _const_spec((1, D_MODEL)),
            _const_spec((D_MODEL, 3 * A_QK)),
            _const_spec((D_MODEL, A_V)),
            _const_spec((D_MODEL, GATE_LANES)),
            _const_spec((3, 3 * A_QK)),
            _const_spec((1, GATE_LANES)),
            _const_spec((1, GATE_LANES)),
        ],
        out_specs=[act_spec(A_QK), act_spec(A_QK), act_spec(A_V), act_spec(A_V),
                   pl.BlockSpec((1, tm, GATE_LANES), lambda b, i: (b, i, 0)),
                   pl.BlockSpec((1, 4 * A_HEADS, tm), lambda b, i: (b, 0, i))],
        out_shape=[act(A_QK), act(A_QK), act(A_V), act(A_V),
                   jax.ShapeDtypeStruct((bsz, seq, GATE_LANES), F32),
                   jax.ShapeDtypeStruct((bsz, 4 * A_HEADS, seq), F32)],
        compiler_params=_params("parallel", "parallel"),
        name="gdn_project",
    )(x, x, x, g_pre, wqkv, wz, wab, conv_w, alog_row, dtb_row)


GDN_HPB = 4
GDN_BLK = 256


def _gdn_direction(q_ref, k_ref, v_ref, gcol_ref, grow_ref, s_ref, o_ref, hh, head, reverse):
    n = GDN_BLK
    hcols = slice(hh * A_DK, (hh + 1) * A_DK)
    q = q_ref[0, :, hcols].astype(F32)
    k = k_ref[0, :, hcols].astype(F32)
    v = v_ref[0, :, hcols].astype(F32)
    d = 1 if reverse else 0
    gcol = gcol_ref[0]
    lane = lax.broadcasted_iota(jnp.int32, (1, GATE_LANES), 1)
    pick = lambda c: jnp.sum(jnp.where(lane == c, gcol, 0.0), axis=-1, keepdims=True)
    gc = pick(d * 16 + head)
    beta = pick(d * 16 + 8 + head)
    gr = grow_ref[0, pl.ds(d * 16 + head, 1), :]

    ri = lax.broadcasted_iota(jnp.int32, (n, n), 0)
    ci = lax.broadcasted_iota(jnp.int32, (n, n), 1)
    same = (ri // A_CHUNK) == (ci // A_CHUNK)
    incl = jnp.logical_and(same, (ri <= ci) if reverse else (ri >= ci))
    strict = jnp.logical_and(incl, ri != ci)
    decay = jnp.where(incl, jnp.exp(jnp.where(incl, gc - gr, 0.0)), 0.0)

    kb = k * beta
    kb16 = kb.astype(BF16)
    k16 = k.astype(BF16)
    kk = _dot_nt(kb16, k16)
    qk = _dot_nt(q.astype(BF16), k16)
    yield
    a = jnp.where(strict, kk * decay, 0.0)
    attn16 = (qk * decay).astype(BF16)

    eye = jnp.where(ri == ci, 1.0, 0.0)
    t_inv = eye - a
    a16 = a.astype(BF16)
    a_pow = _dot(a16, a16)
    yield
    s = 2
    while s < A_CHUNK:
        ap16 = a_pow.astype(BF16)
        upd = _dot(t_inv.astype(BF16), ap16)
        s *= 2
        if s < A_CHUNK:
            a_pow = _dot(ap16, ap16)
        yield
        t_inv = t_inv + upd

    eg = jnp.exp(gc)
    uw = _dot(t_inv.astype(BF16), jnp.concatenate([v * beta, kb * eg], axis=1).astype(BF16))
    yield
    u = uw[:, :A_DV]
    w = uw[:, A_DV:]
    qd = q * eg

    state = s_ref[hh]
    n_chunks = n // A_CHUNK
    order = range(n_chunks - 1, -1, -1) if reverse else range(n_chunks)
    for c in order:
        r0 = c * A_CHUNK
        rows = slice(r0, r0 + A_CHUNK)
        last = r0 if reverse else r0 + A_CHUNK - 1
        g_last = gc[last:last + 1, :]
        s16 = state.astype(BF16)
        wq = _dot(jnp.concatenate([w[rows], qd[rows]], axis=0).astype(BF16), s16)
        yield
        v_new = u[rows] - wq[:A_CHUNK]
        vn16 = v_new.astype(BF16)
        intra = _dot(attn16[rows, r0:r0 + A_CHUNK], vn16)
        kd = k[rows] * jnp.exp(g_last - gc[rows])
        s_add = _dot_tn(kd.astype(BF16), vn16)
        yield
        o_ref[0, rows, hcols] = (wq[A_CHUNK:] + intra).astype(BF16)
        state = state * jnp.exp(g_last) + s_add
    s_ref[hh] = state


def _gdn_core_kernel(qf_ref, kf_ref, vf_ref, gcf_ref, grf_ref, qb_ref, kb_ref, vb_ref, gcb_ref, grb_ref,
                     of_ref, ob_ref, sf_ref, sb_ref):
    @pl.when(pl.program_id(2) == 0)
    def _():
        sf_ref[...] = jnp.zeros_like(sf_ref)
        sb_ref[...] = jnp.zeros_like(sb_ref)

    chains = []
    for hh in range(GDN_HPB):
        head = pl.program_id(1) * GDN_HPB + hh
        chains.append(_gdn_direction(qf_ref, kf_ref, vf_ref, gcf_ref, grf_ref, sf_ref, of_ref, hh, head, False))
        chains.append(_gdn_direction(qb_ref, kb_ref, vb_ref, gcb_ref, grb_ref, sb_ref, ob_ref, hh, head, True))
    while chains:
        chains = [c for c in chains if next(c, True) is None]


def gdn_core(q, k, v, gcol, grow):
    bsz, seq, _ = q.shape
    n = GDN_BLK
    nb = seq // n
    w = GDN_HPB * A_DK
    fwd = lambda b, h, j: (b, j, h)
    bwd = lambda b, h, j: (b, nb - 1 - j, h)
    specs = []
    for im in (fwd, bwd):
        specs += [pl.BlockSpec((1, n, w), im), pl.BlockSpec((1, n, w), im), pl.BlockSpec((1, n, w), im),
                  pl.BlockSpec((1, n, GATE_LANES), (lambda im: lambda b, h, j: (b, im(b, h, j)[1], 0))(im)),
                  pl.BlockSpec((1, 4 * A_HEADS, n), (lambda im: lambda b, h, j: (b, 0, im(b, h, j)[1]))(im))]
    out = jax.ShapeDtypeStruct((bsz, seq, A_V), BF16)
    return pl.pallas_call(
        _gdn_core_kernel,
        grid=(bsz, A_HEADS // GDN_HPB, nb),
        in_specs=specs,
        out_specs=[pl.BlockSpec((1, n, w), fwd), pl.BlockSpec((1, n, w), bwd)],
        out_shape=[out, out],
        scratch_shapes=[pltpu.VMEM((GDN_HPB, A_DK, A_DV), F32), pltpu.VMEM((GDN_HPB, A_DK, A_DV), F32)],
        compiler_params=_params("parallel", "parallel", "arbitrary"),
        name="gdn_core",
    )(q, k, v, gcol, grow, q, k, v, gcol, grow)


OUT_TM = 512


def _gdn_out_kernel(of_ref, ob_ref, z_ref, x_ref, nw_ref, wo_ref, g_ref, y_ref):
    nw = nw_ref[...]
    parts = []
    for h in range(A_HEADS):
        cols = slice(h * A_DV, (h + 1) * A_DV)
        o = of_ref[:, cols].astype(F32) + ob_ref[:, cols].astype(F32)
        o = o * lax.rsqrt(jnp.mean(o * o, axis=-1, keepdims=True) + EPS) * nw
        parts.append((o * _silu(z_ref[:, cols].astype(F32))).astype(BF16))
    hout = _dot(jnp.concatenate(parts, axis=1), wo_ref[...])
    y_ref[...] = x_ref[...] + _rms(hout, g_ref[...])


def gdn_output(o_f, o_b, z, x2, norm_w, w_out, g_post):
    t = x2.shape[0]
    tm = OUT_TM
    row = lambda c: pl.BlockSpec((tm, c), lambda i: (i, 0))
    return pl.pallas_call(
        _gdn_out_kernel,
        grid=(t // tm,),
        in_specs=[row(A_V), row(A_V), row(A_V), row(D_MODEL), _const_spec((1, A_DV)),
                  _const_spec((A_V, D_MODEL)), _const_spec((1, D_MODEL))],
        out_specs=row(D_MODEL),
        out_shape=jax.ShapeDtypeStruct((t, D_MODEL), F32),
        compiler_params=_params("parallel"),
        name="gdn_output",
    )(o_f, o_b, z, x2, norm_w, w_out, g_post)


def gdn_mixer_block(x, g_pre, g_post, w_in, conv_w, a_log, dt_bias, norm_w, w_out):
    bsz, seq, _ = x.shape
    n_qkv = 2 * A_QK + A_V
    wqkv = w_in[:, :n_qkv].astype(BF16)
    wz = w_in[:, n_qkv:n_qkv + A_V].astype(BF16)
    pad = GATE_LANES - 4 * A_HEADS
    wab = jnp.pad(w_in[:, n_qkv + A_V:], ((0, 0), (0, pad))).astype(BF16)
    zeros = jnp.zeros((2, A_HEADS), F32)
    alog_row = jnp.pad(jnp.stack([a_log, zeros], axis=1).reshape(1, 4 * A_HEADS), ((0, 0), (0, pad)))
    dtb_row = jnp.pad(jnp.stack([dt_bias, zeros], axis=1).reshape(1, 4 * A_HEADS), ((0, 0), (0, pad)))
    q, k, v, z, gcol, grow = gdn_project(x, g_pre.reshape(1, -1), wqkv, wz, wab, conv_w, alog_row, dtb_row)
    o_f, o_b = gdn_core(q, k, v, gcol, grow)
    t = bsz * seq
    y = gdn_output(o_f.reshape(t, A_V), o_b.reshape(t, A_V), z.reshape(t, A_V), x.reshape(t, D_MODEL),
                   norm_w.reshape(1, A_DV), w_out.astype(BF16), g_post.reshape(1, -1))
    return y.reshape(bsz, seq, D_MODEL)


RET_BLK = 256
DECAY_ROWS = SUBLANE


def _log_gamma(dl_ref):
    return jnp.log1p(-jnp.exp2(dl_ref[...]))


def _ret_proj_kernel(x_ref, g_ref, w_ref, cos_ref, sin_ref, dl_ref,
                     q_ref, qf_ref, qb_ref, k_ref, kf_ref, kb_ref, v_ref, gate_ref, *, tm):
    xn = _rms(x_ref[0], g_ref[...]).astype(BF16)
    cos = cos_ref[...]
    sin = sin_ref[...]
    nq = B_HEADS * B_DK
    nv = B_HEADS * B_DV
    half = B_DK // 2
    n = RET_BLK
    log_gamma = _log_gamma(dl_ref)
    pos = (lax.broadcasted_iota(jnp.int32, (tm, half), 0) % n).astype(F32)
    for part in range(2):
        p = _dot(xn, w_ref[:, part * nq:(part + 1) * nq])
        for h in range(B_HEADS):
            lgf = log_gamma[0:1, h:h + 1]
            lgb = log_gamma[1:2, h:h + 1]
            x1 = p[:, h * B_DK:h * B_DK + half]
            x2 = p[:, h * B_DK + half:(h + 1) * B_DK]
            if part == 0:
                variants = ((q_ref, None), (qf_ref, jnp.exp(lgf * (pos + 1.0))), (qb_ref, jnp.exp(lgb * (n - pos))))
                scale = 1.0
            else:
                variants = ((k_ref, None), (kf_ref, jnp.exp(lgf * (n - 1.0 - pos))), (kb_ref, jnp.exp(lgb * pos)))
                scale = B_DK ** -0.5
            r1 = (x1 * cos - x2 * sin) * scale
            r2 = (x2 * cos + x1 * sin) * scale
            for ref, fac in variants:
                ref[0, :, h * B_DK:h * B_DK + half] = (r1 if fac is None else r1 * fac).astype(BF16)
                ref[0, :, h * B_DK + half:(h + 1) * B_DK] = (r2 if fac is None else r2 * fac).astype(BF16)
    v_ref[0] = _dot(xn, w_ref[:, 2 * nq:2 * nq + nv]).astype(BF16)
    gate_ref[0] = _dot(xn, w_ref[:, 2 * nq + nv:]).astype(BF16)


def ret_project(x, g_pre, w_in, cos, sin, decay_logit_tile):
    bsz, seq, _ = x.shape
    tm = PROJ_TM
    nq = B_HEADS * B_DK
    nv = B_HEADS * B_DV
    act = lambda c: jax.ShapeDtypeStruct((bsz, seq, c), BF16)
    act_spec = lambda c: pl.BlockSpec((1, tm, c), lambda b, i: (b, i, 0))
    tab_spec = pl.BlockSpec((tm, B_DK // 2), lambda b, i: (i, 0))
    return pl.pallas_call(
        functools.partial(_ret_proj_kernel, tm=tm),
        grid=(bsz, seq // tm),
        in_specs=[act_spec(D_MODEL), _const_spec((1, D_MODEL)), _const_spec((D_MODEL, 2 * nq + 2 * nv)),
                  tab_spec, tab_spec, _const_spec((DECAY_ROWS, LANE))],
        out_specs=[act_spec(nq)] * 6 + [act_spec(nv), act_spec(nv)],
        out_shape=[act(nq)] * 6 + [act(nv), act(nv)],
        compiler_params=_params("parallel", "parallel"),
        name="ret_project",
    )(x, g_pre, w_in, cos, sin, decay_logit_tile)


def _ret_forward(q_ref, k_ref, qx_ref, kz_ref, v_ref, o_ref, r_ref, mask_ref, h, g_blk):
    qcols = slice(h * B_DK, (h + 1) * B_DK)
    vcols = slice(h * B_DV, (h + 1) * B_DV)
    v = v_ref[0, :, vcols]
    scores = _dot_nt(q_ref[0, :, qcols], k_ref[0, :, qcols])
    r = r_ref[h]
    cross = _dot(qx_ref[0, :, qcols], r.astype(BF16))
    r_add = _dot_tn(kz_ref[0, :, qcols], v)
    yield
    inner = _dot((scores * mask_ref[h]).astype(BF16), v)
    r_ref[h] = r * g_blk + r_add
    yield
    o_ref[0, :, vcols] = (inner + cross).astype(BF16)


def _ret_backward(qx_ref, kz_ref, v_ref, o_ref, r_ref, h, g_blk):
    qcols = slice(h * B_DK, (h + 1) * B_DK)
    vcols = slice(h * B_DV, (h + 1) * B_DV)
    r = r_ref[h]
    cross = _dot(qx_ref[0, :, qcols], r.astype(BF16))
    r_add = _dot_tn(kz_ref[0, :, qcols], v_ref[0, :, vcols])
    yield
    o_ref[0, :, vcols] = cross.astype(BF16)
    r_ref[h] = r * g_blk + r_add


def _ret_core_kernel(dl_ref, q_ref, k_ref, qxf_ref, kzf_ref, vf_ref, qxb_ref, kzb_ref, vb_ref,
                     of_ref, ob_ref, rf_ref, rb_ref, mask_ref):
    n = RET_BLK
    log_gamma = _log_gamma(dl_ref)

    @pl.when(pl.program_id(1) == 0)
    def _():
        rf_ref[...] = jnp.zeros_like(rf_ref)
        rb_ref[...] = jnp.zeros_like(rb_ref)
        ri = lax.broadcasted_iota(jnp.int32, (n, n), 0)
        ci = lax.broadcasted_iota(jnp.int32, (n, n), 1)
        dist = (ri - ci).astype(F32)
        for h in range(B_HEADS):
            lgf = log_gamma[0:1, h:h + 1]
            lgb = log_gamma[1:2, h:h + 1]
            mask_ref[h] = (jnp.where(dist >= 0, jnp.exp(lgf * jnp.maximum(dist, 0.0)), 0.0)
                           + jnp.where(dist <= 0, jnp.exp(lgb * jnp.maximum(-dist, 0.0)), 0.0))

    g_blk = jnp.exp(log_gamma * float(n))
    chains = []
    for h in range(B_HEADS):
        chains.append(_ret_forward(q_ref, k_ref, qxf_ref, kzf_ref, vf_ref, of_ref, rf_ref, mask_ref, h,
                                   g_blk[0:1, h:h + 1]))
        chains.append(_ret_backward(qxb_ref, kzb_ref, vb_ref, ob_ref, rb_ref, h, g_blk[1:2, h:h + 1]))
    while chains:
        chains = [c for c in chains if next(c, True) is None]


def ret_core(decay_logit_tile, q, qxf, qxb, k, kzf, kzb, v):
    bsz, seq, _ = q.shape
    n = RET_BLK
    nb = seq // n
    nq = B_HEADS * B_DK
    nv = B_HEADS * B_DV
    fwd = lambda b, j: (b, j, 0)
    bwd = lambda b, j: (b, nb - 1 - j, 0)
    qk = lambda im: pl.BlockSpec((1, n, nq), im)
    val = lambda im: pl.BlockSpec((1, n, nv), im)
    out = jax.ShapeDtypeStruct((bsz, seq, nv), BF16)
    state = pltpu.VMEM((B_HEADS, B_DK, B_DV), F32)
    return pl.pallas_call(
        _ret_core_kernel,
        grid=(bsz, nb),
        in_specs=[_const_spec((DECAY_ROWS, LANE)), qk(fwd), qk(fwd), qk(fwd), qk(fwd), val(fwd),
                  qk(bwd), qk(bwd), val(bwd)],
        out_specs=[val(fwd), val(bwd)],
        out_shape=[out, out],
        scratch_shapes=[state, state, pltpu.VMEM((B_HEADS, n, n), F32)],
        compiler_params=_params("parallel", "arbitrary"),
        name="ret_core",
    )(decay_logit_tile, q, k, qxf, kzf, v, qxb, kzb, v)


def _ret_out_kernel(of_ref, ob_ref, gate_ref, x_ref, gn_ref, wo_ref, g_ref, y_ref):
    parts = []
    for h in range(B_HEADS):
        cols = slice(h * B_DV, (h + 1) * B_DV)
        o = of_ref[:, cols].astype(F32) + ob_ref[:, cols].astype(F32)
        oc = o - jnp.mean(o, axis=-1, keepdims=True)
        on = oc * lax.rsqrt(jnp.mean(oc * oc, axis=-1, keepdims=True) + EPS) * gn_ref[:, cols]
        parts.append((_silu(gate_ref[:, cols].astype(F32)) * on).astype(BF16))
    hout = _dot(jnp.concatenate(parts, axis=1), wo_ref[...])
    y_ref[...] = x_ref[...] + _rms(hout, g_ref[...])


def ret_output(o_f, o_b, gate, x2, gn_w, w_out, g_post):
    t = x2.shape[0]
    tm = OUT_TM
    nv = B_HEADS * B_DV
    row = lambda c: pl.BlockSpec((tm, c), lambda i: (i, 0))
    return pl.pallas_call(
        _ret_out_kernel,
        grid=(t // tm,),
        in_specs=[row(nv), row(nv), row(nv), row(D_MODEL), _const_spec((1, nv)),
                  _const_spec((nv, D_MODEL)), _const_spec((1, D_MODEL))],
        out_specs=row(D_MODEL),
        out_shape=jax.ShapeDtypeStruct((t, D_MODEL), F32),
        compiler_params=_params("parallel"),
        name="ret_output",
    )(o_f, o_b, gate, x2, gn_w, w_out, g_post)


def _rope_tables(seq):
    inv = ROPE_BASE ** (-np.arange(0, B_DK, 2, dtype=np.float64) / B_DK)
    ang = np.arange(seq, dtype=np.float64)[:, None] * inv[None, :]
    return jnp.asarray(np.cos(ang), F32), jnp.asarray(np.sin(ang), F32)


def retnet_mixer_block(x, g_pre, g_post, w_in, decay_logit, gn_w, w_out):
    bsz, seq, _ = x.shape
    cos, sin = _rope_tables(seq)
    dl = jnp.pad(decay_logit, ((0, DECAY_ROWS - 2), (0, LANE - B_HEADS)), constant_values=-1.0)
    q, qxf, qxb, k, kzf, kzb, v, gate = ret_project(x, g_pre.reshape(1, -1), w_in.astype(BF16), cos, sin, dl)
    o_f, o_b = ret_core(dl, q, qxf, qxb, k, kzf, kzb, v)
    t = bsz * seq
    nv = B_HEADS * B_DV
    y = ret_output(o_f.reshape(t, nv), o_b.reshape(t, nv), gate.reshape(t, nv), x.reshape(t, D_MODEL),
                   gn_w.reshape(1, nv), w_out.astype(BF16), g_post.reshape(1, -1))
    return y.reshape(bsz, seq, D_MODEL)


DFT_N2 = 128
HY_LANES = 4 * D_MODEL
HY_CB = D_MODEL
FILTER_TL = 256


def _dft_tables(seq):
    n = 2 * seq
    n2 = DFT_N2
    n1 = n // n2
    kk = np.arange(n1, dtype=np.float64)[:, None]
    nn = np.arange(n1 // 2, dtype=np.float64)[None, :]
    ang = 2.0 * np.pi * kk * nn / n1
    cc, ss = np.cos(ang), np.sin(ang)
    m1 = np.block([[cc, ss], [-ss, cc]])
    m1_real = np.concatenate([cc, -ss], axis=0)
    m3 = m1.T / n
    k1 = np.arange(n1, dtype=np.float64)[:, None, None]
    k2 = np.arange(n2, dtype=np.float64)[None, :, None]
    t2 = np.arange(n2, dtype=np.float64)[None, None, :]
    theta = 2.0 * np.pi * (t2 * k2 / n2 + t2 * k1 / n)
    c, s = np.cos(theta), np.sin(theta)
    m2f = np.concatenate([np.concatenate([c, s], axis=2), np.concatenate([-s, c], axis=2)], axis=1)
    m2i = np.transpose(m2f, (0, 2, 1))
    cast = lambda a: jnp.asarray(a, F32).astype(BF16)
    return n1, cast(m1), cast(m1_real), cast(m3), cast(m2f), cast(m2i)


def _hy_proj_kernel(xm_ref, xp_ref, xn_ref, g_ref, w_ref, bin_ref, cw_ref, cb_ref,
                    v_ref, g1_ref, g2_ref, *, tm, n_tiles):
    xne = _normed_with_halo(xm_ref, xp_ref, xn_ref, g_ref[...])
    valid = _halo_valid(tm, n_tiles)
    for part, ref in enumerate((v_ref, g1_ref, g2_ref)):
        cols = slice(part * D_MODEL, (part + 1) * D_MODEL)
        p = jnp.where(valid, _dot(xne, w_ref[:, cols]) + bin_ref[:, cols], 0.0)
        ref[0] = (_conv3(p, cw_ref[:, cols], tm) + cb_ref[:, cols]).astype(BF16)


def hy_project(x, g_pre, w_in, b_in, conv_w, conv_b):
    bsz, seq, _ = x.shape
    tm = PROJ_TM
    nt = seq // tm
    out = jax.ShapeDtypeStruct((bsz, seq, D_MODEL), BF16)
    spec = pl.BlockSpec((1, tm, D_MODEL), lambda b, i: (b, i, 0))
    return pl.pallas_call(
        functools.partial(_hy_proj_kernel, tm=tm, n_tiles=nt),
        grid=(bsz, nt),
        in_specs=_halo_specs(tm, seq) + [
            _const_spec((1, D_MODEL)), _const_spec((D_MODEL, 3 * D_MODEL)), _const_spec((1, 3 * D_MODEL)),
            _const_spec((3, 3 * D_MODEL)), _const_spec((1, 3 * D_MODEL))],
        out_specs=[spec, spec, spec],
        out_shape=[out, out, out],
        compiler_params=_params("parallel", "parallel"),
        name="hy_project",
    )(x, x, x, g_pre, w_in, b_in, conv_w, conv_b)


def _dot_f32(a, b):
    return jnp.dot(a, b, preferred_element_type=F32, precision=lax.Precision.HIGHEST)


def _hy_filter_kernel(f_ref, w1t_ref, w1c_ref, w1s_ref, b1_ref, fr1_ref, w2_ref, b2_ref, fr2_ref, w3_ref,
                      delta_ref, h_ref, *, tl, seq):
    n = (lax.broadcasted_iota(jnp.int32, (tl, 1), 0) + pl.program_id(0) * tl).astype(F32)
    t = n / (seq - 1.0)
    arg = (2.0 * math.pi / seq) * n * f_ref[...]
    pre = (t * w1t_ref[...] + _dot_f32(jnp.cos(arg), w1c_ref[...]) + _dot_f32(-jnp.sin(arg), w1s_ref[...])
           + b1_ref[...])
    h1 = jnp.sin(fr1_ref[...] * pre)
    h2 = jnp.sin(fr2_ref[...] * (_dot_f32(h1, w2_ref[...]) + b2_ref[...]))
    h3 = _dot_f32(h2, w3_ref[...])
    window = jnp.exp(-t * delta_ref[...])
    for s in range(4):
        hs = h3[:, s * D_MODEL:(s + 1) * D_MODEL] * window
        if s >= 2:
            hs = jnp.where(n == 0.0, 0.0, hs)
        h_ref[s] = hs


def hy_filters(seq, f_w1, f_b1, f_fr1, f_w2, f_b2, f_fr2, f_w3):
    bands = (C_EMB - 1) // 2
    fw = C_FILTER_WIDTH
    f = np.zeros((1, LANE), np.float32)
    f[0, :bands] = np.linspace(1e-4, bands - 1, bands, dtype=np.float32)
    deltas = np.abs(np.linspace(C_MIN_DECAY, C_MAX_DECAY, D_MODEL, dtype=np.float32))[None, :]
    pad = ((0, LANE - bands), (0, 0))
    w1c = jnp.pad(f_w1[1:1 + bands], pad)
    w1s = jnp.pad(f_w1[1 + bands:], pad)
    tl = FILTER_TL
    row = lambda a: a.reshape(1, -1)
    args = (jnp.asarray(f), f_w1[0:1], w1c, w1s, row(f_b1), row(f_fr1), f_w2, row(f_b2), row(f_fr2), f_w3,
            jnp.asarray(deltas))
    return pl.pallas_call(
        functools.partial(_hy_filter_kernel, tl=tl, seq=seq),
        grid=(seq // tl,),
        in_specs=[_const_spec(a.shape) for a in args],
        out_specs=pl.BlockSpec((4, tl, D_MODEL), lambda i: (0, i, 0)),
        out_shape=jax.ShapeDtypeStruct((4, seq, D_MODEL), F32),
        compiler_params=_params("parallel"),
        name="hy_filters",
    )(*args)


def _dft1_kernel(u_ref, m1_ref, y_ref):
    p, half, lw = u_ref.shape[1:]
    y_ref[0] = _dot(m1_ref[...], u_ref[0].reshape(p * half, lw).astype(BF16)).astype(BF16)


def dft_stage1(u4, m1):
    g, p, half, width = u4.shape
    rows = m1.shape[0]
    lw = HY_LANES
    return pl.pallas_call(
        _dft1_kernel,
        grid=(g, width // lw),
        in_specs=[pl.BlockSpec((1, p, half, lw), lambda b, j: (b, 0, 0, j)), _const_spec(m1.shape)],
        out_specs=pl.BlockSpec((1, rows, lw), lambda b, j: (b, 0, j)),
        out_shape=jax.ShapeDtypeStruct((g, rows, width), BF16),
        compiler_params=_params("parallel", "parallel"),
        name="dft_stage1",
    )(u4, m1)


def _slab(ref):
    n2, cb = ref.shape[3:]
    return ref[0, :, 0].reshape(2 * n2, cb)


def _filter_spec_kernel(y_ref, m2f_ref, x_ref):
    n2, cb = y_ref.shape[3:]
    x_ref[0, :, 0] = _dot(m2f_ref[0], _slab(y_ref)).reshape(2, n2, cb)


def filter_spectrum(y5, m2f):
    s, _, n1, n2, c = y5.shape
    cb = HY_CB
    blk = pl.BlockSpec((1, 2, 1, n2, cb), lambda k, j, b: (b, 0, k, 0, j))
    return pl.pallas_call(
        _filter_spec_kernel,
        grid=(n1, c // cb, s),
        in_specs=[blk, pl.BlockSpec((1, 2 * n2, 2 * n2), lambda k, j, b: (k, 0, 0))],
        out_specs=blk,
        out_shape=jax.ShapeDtypeStruct(y5.shape, F32),
        compiler_params=_params("parallel", "parallel", "parallel"),
        name="filter_spectrum",
    )(y5, m2f)


def _conv_freq_kernel(y_ref, m2f_ref, m2i_ref, hf_ref, hb_ref, z_ref):
    n2, cb = y_ref.shape[3:]
    x = _dot(m2f_ref[0], _slab(y_ref))
    xr, xi = x[:n2], x[n2:]
    hr = hf_ref[0, 0, 0] + hb_ref[0, 0, 0]
    hi = hf_ref[0, 1, 0] - hb_ref[0, 1, 0]
    prod = jnp.concatenate([xr * hr - xi * hi, xr * hi + xi * hr], axis=0).astype(BF16)
    z_ref[0, :, 0] = _dot(m2i_ref[0], prod).reshape(2, n2, cb).astype(BF16)


def conv_freq(y5, m2f, m2i, hspec, order):
    g, _, n1, n2, c = y5.shape
    cb = HY_CB
    blk = pl.BlockSpec((1, 2, 1, n2, cb), lambda k, j, b: (b, 0, k, 0, j))
    tab = pl.BlockSpec((1, 2 * n2, 2 * n2), lambda k, j, b: (k, 0, 0))
    hblk = lambda s: pl.BlockSpec((1, 2, 1, n2, cb), lambda k, j, b: (s, 0, k, 0, j))
    return pl.pallas_call(
        _conv_freq_kernel,
        grid=(n1, c // cb, g),
        in_specs=[blk, tab, tab, hblk(order), hblk(2 + order)],
        out_specs=blk,
        out_shape=jax.ShapeDtypeStruct(y5.shape, BF16),
        compiler_params=_params("parallel", "parallel", "parallel"),
        name="conv_freq",
    )(y5, m2f, m2i, hspec, hspec)


def _conv_time_kernel(z_ref, u_ref, gate_ref, skip_ref, m3_ref, *rest, chain):
    half, lw = u_ref.shape[2:]
    y = _dot(m3_ref[...], z_ref[0])
    u = u_ref[0].reshape(2 * half, lw).astype(F32)
    out = (gate_ref[0].reshape(2 * half, lw).astype(F32) * (y + skip_ref[...] * u)).astype(BF16)
    if chain:
        m1_ref, o_ref, y_ref = rest
        y_ref[0] = _dot(m1_ref[...], out).astype(BF16)
    else:
        (o_ref,) = rest
    o_ref[0] = out.reshape(2, half, lw)


def conv_time(z3, u4, gate4, skip, m3, m1=None):
    g, _, half, width = u4.shape
    rows = z3.shape[1]
    lw = HY_LANES
    chain = m1 is not None
    sig = pl.BlockSpec((1, 2, half, lw), lambda b, j: (b, 0, 0, j))
    spec = pl.BlockSpec((1, rows, lw), lambda b, j: (b, 0, j))
    in_specs = [spec, sig, sig, _const_spec((1, lw)), _const_spec(m3.shape)]
    out_specs = [sig]
    out_shape = [jax.ShapeDtypeStruct(u4.shape, BF16)]
    args = [z3, u4, gate4, skip, m3]
    if chain:
        in_specs.append(_const_spec(m1.shape))
        out_specs.append(spec)
        out_shape.append(jax.ShapeDtypeStruct(z3.shape, BF16))
        args.append(m1)
    res = pl.pallas_call(
        functools.partial(_conv_time_kernel, chain=chain),
        grid=(g, width // lw),
        in_specs=in_specs,
        out_specs=out_specs,
        out_shape=out_shape,
        compiler_params=_params("parallel", "parallel"),
        name="conv_time_chain" if chain else "conv_time",
    )(*args)
    return res if chain else res[0]


def _hy_out_kernel(z_ref, x_ref, wo_ref, bo_ref, g_ref, y_ref):
    hout = _dot(z_ref[...].astype(BF16), wo_ref[...]) + bo_ref[...]
    y_ref[...] = x_ref[...] + _rms(hout, g_ref[...])


def hy_output(z2, x2, w_out, b_out, g_post):
    t = x2.shape[0]
    tm = OUT_TM
    row = pl.BlockSpec((tm, D_MODEL), lambda i: (i, 0))
    return pl.pallas_call(
        _hy_out_kernel,
        grid=(t // tm,),
        in_specs=[row, row, _const_spec((D_MODEL, D_MODEL)), _const_spec((1, D_MODEL)), _const_spec((1, D_MODEL))],
        out_specs=row,
        out_shape=jax.ShapeDtypeStruct((t, D_MODEL), F32),
        compiler_params=_params("parallel"),
        name="hy_output",
    )(z2, x2, w_out, b_out, g_post)


def hyena_mixer_block(x, g_pre, g_post, w_in, b_in, conv_w, conv_b, f_w1, f_b1, f_fr1, f_w2, f_b2, f_fr2, f_w3,
                      bias_d, w_out, b_out):
    bsz, seq, _ = x.shape
    n1, m1, m1_real, m3, m2f, m2i = _dft_tables(seq)
    n2 = DFT_N2
    width = n2 * D_MODEL
    pairs = bsz // 2
    sig4 = lambda a: a.reshape(pairs, 2, n1 // 2, width)
    five = lambda a: a.reshape(a.shape[0], 2, n1, n2, D_MODEL)

    filt = hy_filters(seq, f_w1, f_b1, f_fr1, f_w2, f_b2, f_fr2, f_w3)
    hspec = filter_spectrum(five(dft_stage1(filt.reshape(4, 1, n1 // 2, width), m1_real)), m2f)

    v, g1, g2 = hy_project(x, g_pre.reshape(1, -1), w_in.astype(BF16), b_in.reshape(1, -1), conv_w,
                           conv_b.reshape(1, -1))
    v4, g14, g24 = sig4(v), sig4(g1), sig4(g2)
    y = dft_stage1(v4, m1)
    z = conv_freq(five(y), m2f, m2i, hspec, 0).reshape(pairs, 2 * n1, width)
    skip = jnp.tile(bias_d, (1, HY_LANES // D_MODEL))
    z1, y = conv_time(z, v4, g14, skip[0:1], m3, m1)
    z = conv_freq(five(y), m2f, m2i, hspec, 1).reshape(pairs, 2 * n1, width)
    z2 = conv_time(z, z1, g24, skip[1:2], m3)
    t = bsz * seq
    out = hy_output(z2.reshape(t, D_MODEL), x.reshape(t, D_MODEL), w_out.astype(BF16), b_out.reshape(1, -1),
                    g_post.reshape(1, -1))
    return out.reshape(bsz, seq, D_MODEL)


def kernel(x_prompt, x_sample, norm_g, ffn_w_gate, ffn_w_up, ffn_w_down, a_w_in, a_conv_w, a_a_log, a_dt_bias,
           a_norm_w, a_w_out, b_w_in, b_decay_logit, b_gn_w, b_w_out, c_w_in, c_b_in, c_conv_w, c_conv_b,
           c_f_w1, c_f_b1, c_f_freq1, c_f_w2, c_f_b2, c_f_freq2, c_f_w3, c_bias_d, c_w_out, c_b_out):
    depth = norm_g.shape[0]
    wg = ffn_w_gate.astype(BF16)
    wu = ffn_w_up.astype(BF16)
    wd = ffn_w_down.astype(BF16)

    def ffn(x, i, which):
        bsz, seq, _ = x.shape
        g = norm_g[i, 4 * which:4 * which + 2]
        return ffn_block(x.reshape(bsz * seq, D_MODEL), g, wg[i, which], wu[i, which],
                         wd[i, which]).reshape(bsz, seq, D_MODEL)

    def mixer(x, i):
        kind, j = i % 3, i // 3
        g_pre, g_post = norm_g[i, 2], norm_g[i, 3]
        if kind == 0:
            return gdn_mixer_block(x, g_pre, g_post, a_w_in[j], a_conv_w[j], a_a_log[j], a_dt_bias[j],
                                   a_norm_w[j], a_w_out[j])
        if kind == 1:
            return retnet_mixer_block(x, g_pre, g_post, b_w_in[j], b_decay_logit[j], b_gn_w[j], b_w_out[j])
        return hyena_mixer_block(x, g_pre, g_post, c_w_in[j], c_b_in[j], c_conv_w[j], c_conv_b[j], c_f_w1[j],
                                 c_f_b1[j], c_f_freq1[j], c_f_w2[j], c_f_b2[j], c_f_freq2[j], c_f_w3[j],
                                 c_bias_d[j], c_w_out[j], c_b_out[j])

    def trunk(x):
        for i in range(depth):
            x = ffn(x, i, 0)
            x = mixer(x, i)
            x = ffn(x, i, 1)
        return x

    return (trunk(x_prompt), trunk(x_sample))
```

```python
import functools
import math

import numpy as np
import jax
import jax.numpy as jnp
from jax import lax
from jax.experimental import pallas as pl
from jax.experimental.pallas import tpu as pltpu

F32 = jnp.float32
BF16 = jnp.bfloat16
EPS = 1e-6

D_MODEL = 1024
D_FF = 2816

A_HEADS, A_DK, A_DV, A_CHUNK = 8, 128, 128, 64
A_QK = A_HEADS * A_DK
A_V = A_HEADS * A_DV
B_HEADS, B_DK, B_DV = 4, 256, 512
ROPE_BASE = 10000.0
C_EMB, C_FILTER_WIDTH = 33, 64
C_TARGET, C_SHORT_DECAY_PCT, C_LONG_DECAY_PCT = 1e-2, 0.3, 1.5
C_MIN_DECAY = math.log(C_TARGET) / C_LONG_DECAY_PCT
C_MAX_DECAY = math.log(C_TARGET) / C_SHORT_DECAY_PCT

V7X_VMEM_BYTES = 64 * 1024 * 1024
VMEM_LIMIT = V7X_VMEM_BYTES - 8 * 1024 * 1024
SUBLANE = 8
LANE = 128
MXU_DIM = 256

HALO = SUBLANE


def _params(*sem):
    return pltpu.CompilerParams(dimension_semantics=sem, vmem_limit_bytes=VMEM_LIMIT)


def _const_spec(shape):
    nd = len(shape)
    return pl.BlockSpec(shape, lambda *_: (0,) * nd, pipeline_mode=pl.Buffered(1))


def _rms(x, g):
    return x * lax.rsqrt(jnp.mean(x * x, axis=-1, keepdims=True) + EPS) * g


def _silu(x):
    return x * jax.nn.sigmoid(x)


def _dot(a, b):
    return jnp.dot(a, b, preferred_element_type=F32)


def _dot_nt(a, b):
    return lax.dot_general(a, b, (((1,), (1,)), ((), ())), preferred_element_type=F32)


def _dot_tn(a, b):
    return lax.dot_general(a, b, (((0,), (0,)), ((), ())), preferred_element_type=F32)


FFN_TM = 512
FFN_CHUNKS = 2


def _ffn_value(x, g, wg_ref, wu_ref, wd_ref):
    xn = _rms(x, g[0:1]).astype(BF16)
    fc = D_FF // FFN_CHUNKS
    y = None
    for c in range(FFN_CHUNKS):
        hg = _dot(xn, wg_ref[:, c * fc:(c + 1) * fc])
        hu = _dot(xn, wu_ref[:, c * fc:(c + 1) * fc])
        act = (_silu(hg) * hu).astype(BF16)
        part = _dot(act, wd_ref[c * fc:(c + 1) * fc, :])
        y = part if y is None else y + part
    return x + 0.5 * _rms(y, g[1:2])


def _ffn_kernel(x_ref, g_ref, wg_ref, wu_ref, wd_ref, o_ref):
    o_ref[...] = _ffn_value(x_ref[...], g_ref[...], wg_ref, wu_ref, wd_ref)


def _tail_ffn_kernel(*refs, tail, n_tail):
    g_ref, wg_ref, wu_ref, wd_ref, o_ref = refs[n_tail:]
    o_ref[...] = _ffn_value(tail(*refs[:n_tail]), g_ref[...], wg_ref, wu_ref, wd_ref)


def _ffn_specs():
    return [_const_spec((2, D_MODEL)), _const_spec((D_MODEL, D_FF)), _const_spec((D_MODEL, D_FF)),
            _const_spec((D_FF, D_MODEL))]


def tail_ffn_block(tail, tail_args, tail_specs, ffn_args, name):
    t = tail_args[0].shape[0]
    tm = FFN_TM
    return pl.pallas_call(
        functools.partial(_tail_ffn_kernel, tail=tail, n_tail=len(tail_args)),
        grid=(t // tm,),
        in_specs=list(tail_specs) + _ffn_specs(),
        out_specs=pl.BlockSpec((tm, D_MODEL), lambda i: (i, 0)),
        out_shape=jax.ShapeDtypeStruct((t, D_MODEL), F32),
        compiler_params=_params("parallel"),
        name=name,
    )(*tail_args, *ffn_args)


def _row_spec(cols):
    return pl.BlockSpec((FFN_TM, cols), lambda i: (i, 0))


def ffn_block(x2, g_pair, wg, wu, wd):
    t = x2.shape[0]
    tm = FFN_TM
    return pl.pallas_call(
        _ffn_kernel,
        grid=(t // tm,),
        in_specs=[_row_spec(D_MODEL)] + _ffn_specs(),
        out_specs=_row_spec(D_MODEL),
        out_shape=jax.ShapeDtypeStruct((t, D_MODEL), F32),
        compiler_params=_params("parallel"),
        name="ffn_block",
    )(x2, g_pair, wg, wu, wd)


PROJ_TM = 512


def _halo_specs(tm, seq):
    per = tm // HALO
    last = seq // HALO - 1
    return [
        pl.BlockSpec((1, tm, D_MODEL), lambda b, i: (b, i, 0)),
        pl.BlockSpec((1, HALO, D_MODEL), lambda b, i: (b, jnp.maximum(i * per - 1, 0), 0)),
        pl.BlockSpec((1, HALO, D_MODEL), lambda b, i: (b, jnp.minimum((i + 1) * per, last), 0)),
    ]


def _normed_with_halo(xm_ref, xp_ref, xn_ref, g):
    xe = jnp.concatenate([xp_ref[0], xm_ref[0], xn_ref[0]], axis=0)
    return _rms(xe, g).astype(BF16)


def _halo_valid(tm, n_tiles):
    i = pl.program_id(1)
    row = lax.broadcasted_iota(jnp.int32, (tm + 2 * HALO, 1), 0)
    before = jnp.logical_and(row < HALO, i == 0)
    after = jnp.logical_and(row >= tm + HALO, i == n_tiles - 1)
    return jnp.logical_not(jnp.logical_or(before, after))


def _conv3(p, cw, tm):
    rows = p.shape[0]
    prev = pltpu.roll(p, 1, 0)[HALO:HALO + tm]
    nxt = pltpu.roll(p, rows - 1, 0)[HALO:HALO + tm]
    return prev * cw[0:1] + p[HALO:HALO + tm] * cw[1:2] + nxt * cw[2:3]


GATE_LANES = LANE


def _chunk_scan(val, pos, size, axis, reverse):
    s = 1
    while s < A_CHUNK:
        if reverse:
            shifted = pltpu.roll(val, size - s, axis)
            val = val + jnp.where(pos < A_CHUNK - s, shifted, 0.0)
        else:
            shifted = pltpu.roll(val, s, axis)
            val = val + jnp.where(pos >= s, shifted, 0.0)
        s *= 2
    return val


def _gdn_proj_kernel(xm_ref, xp_ref, xn_ref, g_ref, wqkv_ref, wz_ref, wab_ref, cw_ref, alog_ref, dtb_ref,
                     q_ref, k_ref, v_ref, z_ref, gcol_ref, grow_ref, *, tm, n_tiles):
    xne = _normed_with_halo(xm_ref, xp_ref, xn_ref, g_ref[...])
    valid = _halo_valid(tm, n_tiles)
    outs = (q_ref, k_ref, v_ref)
    for part in range(3):
        cols = slice(part * A_QK, (part + 1) * A_QK)
        p = jnp.where(valid, _dot(xne, wqkv_ref[:, cols]), 0.0)
        a = _silu(_conv3(p, cw_ref[:, cols], tm))
        if part == 2:
            v_ref[0] = a.astype(BF16)
            continue
        scale = A_DK ** -0.5 if part == 0 else 1.0
        for h in range(A_HEADS):
            ah = a[:, h * A_DK:(h + 1) * A_DK]
            inv = lax.rsqrt(jnp.sum(ah * ah, axis=-1, keepdims=True) + EPS) * scale
            outs[part][0, :, h * A_DK:(h + 1) * A_DK] = (ah * inv).astype(BF16)
    xn_main = xne[HALO:HALO + tm]
    z_ref[0] = _dot(xn_main, wz_ref[...]).astype(BF16)
    ab = _dot(xn_main, wab_ref[...])
    col = lax.broadcasted_iota(jnp.int32, (1, GATE_LANES), 1)
    is_decay = jnp.logical_and(col % 16 < 8, col < 32)
    sp = ab + dtb_ref[...]
    softplus = jnp.maximum(sp, 0.0) + jnp.log(1.0 + jnp.exp(-jnp.abs(sp)))
    val = jnp.where(is_decay, -jnp.exp(alog_ref[...]) * softplus, jax.nn.sigmoid(ab))
    pos = lax.broadcasted_iota(jnp.int32, (tm, 1), 0) % A_CHUNK
    fwd = _chunk_scan(val, pos, tm, 0, False)
    bwd = _chunk_scan(val, pos, tm, 0, True)
    gate = jnp.where(col < 8, fwd, jnp.where(jnp.logical_and(col >= 16, col < 24), bwd, val))
    gcol_ref[0] = gate
    grow_ref[0] = gate.T[0:4 * A_HEADS, :]


def gdn_project(x, g_pre, wqkv, wz, wab, conv_w, alog_row, dtb_row):
    bsz, seq, _ = x.shape
    tm = PROJ_TM
    nt = seq // tm
    act = lambda c: jax.ShapeDtypeStruct((bsz, seq, c), BF16)
    act_spec = lambda c: pl.BlockSpec((1, tm, c), lambda b, i: (b, i, 0))
    return pl.pallas_call(
        functools.partial(_gdn_proj_kernel, tm=tm, n_tiles=nt),
        grid=(bsz, nt),
        in_specs=_halo_specs(tm, seq) + [
            _const_spec((1, D_MODEL)),
            _const_spec((D_MODEL, 3 * A_QK)),
            _const_spec((D_MODEL, A_V)),
            _const_spec((D_MODEL, GATE_LANES)),
            _const_spec((3, 3 * A_QK)),
            _const_spec((1, GATE_LANES)),
            _const_spec((1, GATE_LANES)),
        ],
        out_specs=[act_spec(A_QK), act_spec(A_QK), act_spec(A_V), act_spec(A_V),
                   pl.BlockSpec((1, tm, GATE_LANES), lambda b, i: (b, i, 0)),
                   pl.BlockSpec((1, 4 * A_HEADS, tm), lambda b, i: (b, 0, i))],
        out_shape=[act(A_QK), act(A_QK), act(A_V), act(A_V),
                   jax.ShapeDtypeStruct((bsz, seq, GATE_LANES), F32),
                   jax.ShapeDtypeStruct((bsz, 4 * A_HEADS, seq), F32)],
        compiler_params=_params("parallel", "parallel"),
        name="gdn_project",
    )(x, x, x, g_pre, wqkv, wz, wab, conv_w, alog_row, dtb_row)


GDN_HPB = 8
GDN_BLK = 256


def _gdn_direction(q_ref, k_ref, v_ref, gcol_ref, grow_ref, s_ref, o_ref, hh, head, reverse):
    n = GDN_BLK
    hcols = slice(hh * A_DK, (hh + 1) * A_DK)
    q = q_ref[0, :, hcols].astype(F32)
    k = k_ref[0, :, hcols].astype(F32)
    v = v_ref[0, :, hcols].astype(F32)
    d = 1 if reverse else 0
    gcol = gcol_ref[0]
    lane = lax.broadcasted_iota(jnp.int32, (1, GATE_LANES), 1)
    pick = lambda c: jnp.sum(jnp.where(lane == c, gcol, 0.0), axis=-1, keepdims=True)
    gc = pick(d * 16 + head)
    beta = pick(d * 16 + 8 + head)
    gr = grow_ref[0, pl.ds(d * 16 + head, 1), :]

    ri = lax.broadcasted_iota(jnp.int32, (n, n), 0)
    ci = lax.broadcasted_iota(jnp.int32, (n, n), 1)
    same = (ri // A_CHUNK) == (ci // A_CHUNK)
    incl = jnp.logical_and(same, (ri <= ci) if reverse else (ri >= ci))
    strict = jnp.logical_and(incl, ri != ci)
    decay = jnp.where(incl, jnp.exp(jnp.where(incl, gc - gr, 0.0)), 0.0)

    kb = k * beta
    kb16 = kb.astype(BF16)
    k16 = k.astype(BF16)
    kk = _dot_nt(kb16, k16)
    qk = _dot_nt(q.astype(BF16), k16)
    yield
    a = jnp.where(strict, kk * decay, 0.0)
    attn16 = (qk * decay).astype(BF16)

    n_chunks = n // A_CHUNK
    side_by_side = lambda m: sum(m[c * A_CHUNK:(c + 1) * A_CHUNK] for c in range(n_chunks))
    block_diag16 = lambda m: jnp.where(same, jnp.concatenate([m] * n_chunks, axis=0), 0.0).astype(BF16)
    a_side = side_by_side(a)
    eye_side = (lax.broadcasted_iota(jnp.int32, (A_CHUNK, n), 0)
                == lax.broadcasted_iota(jnp.int32, (A_CHUNK, n), 1) % A_CHUNK)
    t_side = jnp.where(eye_side, 1.0, 0.0) - a_side
    a_pow = _dot(a_side.astype(BF16), a.astype(BF16))
    yield
    s = 2
    while s < A_CHUNK:
        ap16 = block_diag16(a_pow)
        upd = _dot(t_side.astype(BF16), ap16)
        s *= 2
        if s < A_CHUNK:
            a_pow = _dot(a_pow.astype(BF16), ap16)
        yield
        t_side = t_side + upd

    eg = jnp.exp(gc)
    uw = _dot(block_diag16(t_side), jnp.concatenate([v * beta, kb * eg], axis=1).astype(BF16))
    yield
    u = uw[:, :A_DV]
    w = uw[:, A_DV:]
    qd = q * eg

    state = s_ref[hh]
    n_chunks = n // A_CHUNK
    order = range(n_chunks - 1, -1, -1) if reverse else range(n_chunks)
    for c in order:
        r0 = c * A_CHUNK
        rows = slice(r0, r0 + A_CHUNK)
        last = r0 if reverse else r0 + A_CHUNK - 1
        g_last = gc[last:last + 1, :]
        s16 = state.astype(BF16)
        wq = _dot(jnp.concatenate([w[rows], qd[rows]], axis=0).astype(BF16), s16)
        yield
        v_new = u[rows] - wq[:A_CHUNK]
        vn16 = v_new.astype(BF16)
        intra = _dot(attn16[rows, r0:r0 + A_CHUNK], vn16)
        kd = k[rows] * jnp.exp(g_last - gc[rows])
        s_add = _dot_tn(kd.astype(BF16), vn16)
        yield
        o_ref[0, rows, hcols] = (wq[A_CHUNK:] + intra).astype(BF16)
        state = state * jnp.exp(g_last) + s_add
    s_ref[hh] = state


def _gdn_core_kernel(qf_ref, kf_ref, vf_ref, gcf_ref, grf_ref, qb_ref, kb_ref, vb_ref, gcb_ref, grb_ref,
                     of_ref, ob_ref, sf_ref, sb_ref):
    @pl.when(pl.program_id(2) == 0)
    def _():
        sf_ref[...] = jnp.zeros_like(sf_ref)
        sb_ref[...] = jnp.zeros_like(sb_ref)

    chains = []
    for hh in range(GDN_HPB):
        head = hh if GDN_HPB == A_HEADS else pl.program_id(1) * GDN_HPB + hh
        chains.append(_gdn_direction(qf_ref, kf_ref, vf_ref, gcf_ref, grf_ref, sf_ref, of_ref, hh, head, False))
        chains.append(_gdn_direction(qb_ref, kb_ref, vb_ref, gcb_ref, grb_ref, sb_ref, ob_ref, hh, head, True))
    while chains:
        chains = [c for c in chains if next(c, True) is None]


def gdn_core(q, k, v, gcol, grow):
    bsz, seq, _ = q.shape
    n = GDN_BLK
    nb = seq // n
    w = GDN_HPB * A_DK
    fwd = lambda b, h, j: (b, j, h)
    bwd = lambda b, h, j: (b, nb - 1 - j, h)
    specs = []
    for im in (fwd, bwd):
        specs += [pl.BlockSpec((1, n, w), im), pl.BlockSpec((1, n, w), im), pl.BlockSpec((1, n, w), im),
                  pl.BlockSpec((1, n, GATE_LANES), (lambda im: lambda b, h, j: (b, im(b, h, j)[1], 0))(im)),
                  pl.BlockSpec((1, 4 * A_HEADS, n), (lambda im: lambda b, h, j: (b, 0, im(b, h, j)[1]))(im))]
    out = jax.ShapeDtypeStruct((bsz, seq, A_V), BF16)
    return pl.pallas_call(
        _gdn_core_kernel,
        grid=(bsz, A_HEADS // GDN_HPB, nb),
        in_specs=specs,
        out_specs=[pl.BlockSpec((1, n, w), fwd), pl.BlockSpec((1, n, w), bwd)],
        out_shape=[out, out],
        scratch_shapes=[pltpu.VMEM((GDN_HPB, A_DK, A_DV), F32), pltpu.VMEM((GDN_HPB, A_DK, A_DV), F32)],
        compiler_params=_params("parallel", "parallel", "arbitrary"),
        name="gdn_core",
    )(q, k, v, gcol, grow, q, k, v, gcol, grow)


def _gdn_tail(of_ref, ob_ref, z_ref, x_ref, nw_ref, wo_ref, g_ref):
    nw = nw_ref[...]
    parts = []
    for h in range(A_HEADS):
        cols = slice(h * A_DV, (h + 1) * A_DV)
        o = of_ref[:, cols].astype(F32) + ob_ref[:, cols].astype(F32)
        o = o * lax.rsqrt(jnp.mean(o * o, axis=-1, keepdims=True) + EPS) * nw
        parts.append((o * _silu(z_ref[:, cols].astype(F32))).astype(BF16))
    hout = _dot(jnp.concatenate(parts, axis=1), wo_ref[...])
    return x_ref[...] + _rms(hout, g_ref[...])


def gdn_mixer_block(x, g_pre, g_post, w_in, conv_w, a_log, dt_bias, norm_w, w_out, ffn_args):
    bsz, seq, _ = x.shape
    n_qkv = 2 * A_QK + A_V
    wqkv = w_in[:, :n_qkv].astype(BF16)
    wz = w_in[:, n_qkv:n_qkv + A_V].astype(BF16)
    pad = GATE_LANES - 4 * A_HEADS
    wab = jnp.pad(w_in[:, n_qkv + A_V:], ((0, 0), (0, pad))).astype(BF16)
    zeros = jnp.zeros((2, A_HEADS), F32)
    alog_row = jnp.pad(jnp.stack([a_log, zeros], axis=1).reshape(1, 4 * A_HEADS), ((0, 0), (0, pad)))
    dtb_row = jnp.pad(jnp.stack([dt_bias, zeros], axis=1).reshape(1, 4 * A_HEADS), ((0, 0), (0, pad)))
    q, k, v, z, gcol, grow = gdn_project(x, g_pre.reshape(1, -1), wqkv, wz, wab, conv_w, alog_row, dtb_row)
    o_f, o_b = gdn_core(q, k, v, gcol, grow)
    t = bsz * seq
    args = (o_f.reshape(t, A_V), o_b.reshape(t, A_V), z.reshape(t, A_V), x.reshape(t, D_MODEL),
            norm_w.reshape(1, A_DV), w_out.astype(BF16), g_post.reshape(1, -1))
    specs = [_row_spec(A_V), _row_spec(A_V), _row_spec(A_V), _row_spec(D_MODEL), _const_spec((1, A_DV)),
             _const_spec((A_V, D_MODEL)), _const_spec((1, D_MODEL))]
    return tail_ffn_block(_gdn_tail, args, specs, ffn_args, "gdn_tail_ffn").reshape(bsz, seq, D_MODEL)


RET_BLK = 256
DECAY_ROWS = SUBLANE


def _log_gamma(dl_ref):
    return jnp.log1p(-jnp.exp2(dl_ref[...]))


def _ret_proj_kernel(x_ref, g_ref, w_ref, cos_ref, sin_ref, dl_ref,
                     q_ref, qf_ref, qb_ref, k_ref, kf_ref, kb_ref, v_ref, gate_ref, *, tm):
    xn = _rms(x_ref[0], g_ref[...]).astype(BF16)
    cos = cos_ref[...]
    sin = sin_ref[...]
    nq = B_HEADS * B_DK
    nv = B_HEADS * B_DV
    half = B_DK // 2
    n = RET_BLK
    log_gamma = _log_gamma(dl_ref)
    pos = (lax.broadcasted_iota(jnp.int32, (tm, half), 0) % n).astype(F32)
    for part in range(2):
        p = _dot(xn, w_ref[:, part * nq:(part + 1) * nq])
        for h in range(B_HEADS):
            lgf = log_gamma[0:1, h:h + 1]
            lgb = log_gamma[1:2, h:h + 1]
            x1 = p[:, h * B_DK:h * B_DK + half]
            x2 = p[:, h * B_DK + half:(h + 1) * B_DK]
            if part == 0:
                variants = ((q_ref, None), (qf_ref, jnp.exp(lgf * (pos + 1.0))), (qb_ref, jnp.exp(lgb * (n - pos))))
                scale = 1.0
            else:
                variants = ((k_ref, None), (kf_ref, jnp.exp(lgf * (n - 1.0 - pos))), (kb_ref, jnp.exp(lgb * pos)))
                scale = B_DK ** -0.5
            r1 = (x1 * cos - x2 * sin) * scale
            r2 = (x2 * cos + x1 * sin) * scale
            for ref, fac in variants:
                ref[0, :, h * B_DK:h * B_DK + half] = (r1 if fac is None else r1 * fac).astype(BF16)
                ref[0, :, h * B_DK + half:(h + 1) * B_DK] = (r2 if fac is None else r2 * fac).astype(BF16)
    v_ref[0] = _dot(xn, w_ref[:, 2 * nq:2 * nq + nv]).astype(BF16)
    gate_ref[0] = _dot(xn, w_ref[:, 2 * nq + nv:]).astype(BF16)


def ret_project(x, g_pre, w_in, cos, sin, decay_logit_tile):
    bsz, seq, _ = x.shape
    tm = PROJ_TM
    nq = B_HEADS * B_DK
    nv = B_HEADS * B_DV
    act = lambda c: jax.ShapeDtypeStruct((bsz, seq, c), BF16)
    act_spec = lambda c: pl.BlockSpec((1, tm, c), lambda b, i: (b, i, 0))
    tab_spec = pl.BlockSpec((tm, B_DK // 2), lambda b, i: (i, 0))
    return pl.pallas_call(
        functools.partial(_ret_proj_kernel, tm=tm),
        grid=(bsz, seq // tm),
        in_specs=[act_spec(D_MODEL), _const_spec((1, D_MODEL)), _const_spec((D_MODEL, 2 * nq + 2 * nv)),
                  tab_spec, tab_spec, _const_spec((DECAY_ROWS, LANE))],
        out_specs=[act_spec(nq)] * 6 + [act_spec(nv), act_spec(nv)],
        out_shape=[act(nq)] * 6 + [act(nv), act(nv)],
        compiler_params=_params("parallel", "parallel"),
        name="ret_project",
    )(x, g_pre, w_in, cos, sin, decay_logit_tile)


def _ret_forward(q_ref, k_ref, qx_ref, kz_ref, v_ref, o_ref, r_ref, mask_ref, h, g_blk):
    qcols = slice(h * B_DK, (h + 1) * B_DK)
    vcols = slice(h * B_DV, (h + 1) * B_DV)
    v = v_ref[0, :, vcols]
    scores = _dot_nt(q_ref[0, :, qcols], k_ref[0, :, qcols])
    r = r_ref[h]
    cross = _dot(qx_ref[0, :, qcols], r.astype(BF16))
    r_add = _dot_tn(kz_ref[0, :, qcols], v)
    yield
    inner = _dot((scores * mask_ref[h]).astype(BF16), v)
    r_ref[h] = r * g_blk + r_add
    yield
    o_ref[0, :, vcols] = (inner + cross).astype(BF16)


def _ret_backward(qx_ref, kz_ref, v_ref, o_ref, r_ref, h, g_blk):
    qcols = slice(h * B_DK, (h + 1) * B_DK)
    vcols = slice(h * B_DV, (h + 1) * B_DV)
    r = r_ref[h]
    cross = _dot(qx_ref[0, :, qcols], r.astype(BF16))
    r_add = _dot_tn(kz_ref[0, :, qcols], v_ref[0, :, vcols])
    yield
    o_ref[0, :, vcols] = cross.astype(BF16)
    r_ref[h] = r * g_blk + r_add


def _ret_core_kernel(dl_ref, q_ref, k_ref, qxf_ref, kzf_ref, vf_ref, qxb_ref, kzb_ref, vb_ref,
                     of_ref, ob_ref, rf_ref, rb_ref, mask_ref):
    n = RET_BLK
    log_gamma = _log_gamma(dl_ref)

    @pl.when(pl.program_id(1) == 0)
    def _():
        rf_ref[...] = jnp.zeros_like(rf_ref)
        rb_ref[...] = jnp.zeros_like(rb_ref)
        ri = lax.broadcasted_iota(jnp.int32, (n, n), 0)
        ci = lax.broadcasted_iota(jnp.int32, (n, n), 1)
        dist = (ri - ci).astype(F32)
        for h in range(B_HEADS):
            lgf = log_gamma[0:1, h:h + 1]
            lgb = log_gamma[1:2, h:h + 1]
            mask_ref[h] = (jnp.where(dist >= 0, jnp.exp(lgf * jnp.maximum(dist, 0.0)), 0.0)
                           + jnp.where(dist <= 0, jnp.exp(lgb * jnp.maximum(-dist, 0.0)), 0.0))

    g_blk = jnp.exp(log_gamma * float(n))
    chains = []
    for h in range(B_HEADS):
        chains.append(_ret_forward(q_ref, k_ref, qxf_ref, kzf_ref, vf_ref, of_ref, rf_ref, mask_ref, h,
                                   g_blk[0:1, h:h + 1]))
        chains.append(_ret_backward(qxb_ref, kzb_ref, vb_ref, ob_ref, rb_ref, h, g_blk[1:2, h:h + 1]))
    while chains:
        chains = [c for c in chains if next(c, True) is None]


def ret_core(decay_logit_tile, q, qxf, qxb, k, kzf, kzb, v):
    bsz, seq, _ = q.shape
    n = RET_BLK
    nb = seq // n
    nq = B_HEADS * B_DK
    nv = B_HEADS * B_DV
    fwd = lambda b, j: (b, j, 0)
    bwd = lambda b, j: (b, nb - 1 - j, 0)
    qk = lambda im: pl.BlockSpec((1, n, nq), im)
    val = lambda im: pl.BlockSpec((1, n, nv), im)
    out = jax.ShapeDtypeStruct((bsz, seq, nv), BF16)
    state = pltpu.VMEM((B_HEADS, B_DK, B_DV), F32)
    return pl.pallas_call(
        _ret_core_kernel,
        grid=(bsz, nb),
        in_specs=[_const_spec((DECAY_ROWS, LANE)), qk(fwd), qk(fwd), qk(fwd), qk(fwd), val(fwd),
                  qk(bwd), qk(bwd), val(bwd)],
        out_specs=[val(fwd), val(bwd)],
        out_shape=[out, out],
        scratch_shapes=[state, state, pltpu.VMEM((B_HEADS, n, n), F32)],
        compiler_params=_params("parallel", "arbitrary"),
        name="ret_core",
    )(decay_logit_tile, q, k, qxf, kzf, v, qxb, kzb, v)


def _ret_tail(of_ref, ob_ref, gate_ref, x_ref, gn_ref, wo_ref, g_ref):
    parts = []
    for h in range(B_HEADS):
        cols = slice(h * B_DV, (h + 1) * B_DV)
        o = of_ref[:, cols].astype(F32) + ob_ref[:, cols].astype(F32)
        oc = o - jnp.mean(o, axis=-1, keepdims=True)
        on = oc * lax.rsqrt(jnp.mean(oc * oc, axis=-1, keepdims=True) + EPS) * gn_ref[:, cols]
        parts.append((_silu(gate_ref[:, cols].astype(F32)) * on).astype(BF16))
    hout = _dot(jnp.concatenate(parts, axis=1), wo_ref[...])
    return x_ref[...] + _rms(hout, g_ref[...])


def _rope_tables(seq):
    inv = ROPE_BASE ** (-np.arange(0, B_DK, 2, dtype=np.float64) / B_DK)
    ang = np.arange(seq, dtype=np.float64)[:, None] * inv[None, :]
    return jnp.asarray(np.cos(ang), F32), jnp.asarray(np.sin(ang), F32)


def retnet_mixer_block(x, g_pre, g_post, w_in, decay_logit, gn_w, w_out, ffn_args):
    bsz, seq, _ = x.shape
    cos, sin = _rope_tables(seq)
    dl = jnp.pad(decay_logit, ((0, DECAY_ROWS - 2), (0, LANE - B_HEADS)), constant_values=-1.0)
    q, qxf, qxb, k, kzf, kzb, v, gate = ret_project(x, g_pre.reshape(1, -1), w_in.astype(BF16), cos, sin, dl)
    o_f, o_b = ret_core(dl, q, qxf, qxb, k, kzf, kzb, v)
    t = bsz * seq
    nv = B_HEADS * B_DV
    args = (o_f.reshape(t, nv), o_b.reshape(t, nv), gate.reshape(t, nv), x.reshape(t, D_MODEL),
            gn_w.reshape(1, nv), w_out.astype(BF16), g_post.reshape(1, -1))
    specs = [_row_spec(nv), _row_spec(nv), _row_spec(nv), _row_spec(D_MODEL), _const_spec((1, nv)),
             _const_spec((nv, D_MODEL)), _const_spec((1, D_MODEL))]
    return tail_ffn_block(_ret_tail, args, specs, ffn_args, "ret_tail_ffn").reshape(bsz, seq, D_MODEL)


DFT_N2 = 128
HY_LANES = 4 * D_MODEL
HY_CB = D_MODEL
FILTER_TL = 256


def _dft_tables(seq):
    n = 2 * seq
    n2 = DFT_N2
    n1 = n // n2
    kk = np.arange(n1, dtype=np.float64)[:, None]
    nn = np.arange(n1 // 2, dtype=np.float64)[None, :]
    ang = 2.0 * np.pi * kk * nn / n1
    cc, ss = np.cos(ang), np.sin(ang)
    m1 = np.block([[cc, ss], [-ss, cc]])
    m1_real = np.concatenate([cc, -ss], axis=0)
    m3 = m1.T / n
    k1 = np.arange(n1, dtype=np.float64)[:, None, None]
    k2 = np.arange(n2, dtype=np.float64)[None, :, None]
    t2 = np.arange(n2, dtype=np.float64)[None, None, :]
    theta = 2.0 * np.pi * (t2 * k2 / n2 + t2 * k1 / n)
    c, s = np.cos(theta), np.sin(theta)
    m2f = np.concatenate([np.concatenate([c, s], axis=2), np.concatenate([-s, c], axis=2)], axis=1)
    m2i = np.transpose(m2f, (0, 2, 1))
    cast = lambda a: jnp.asarray(a, F32).astype(BF16)
    return n1, cast(m1), cast(m1_real), cast(m3), cast(m2f), cast(m2i)


def _hy_proj_kernel(xm_ref, xp_ref, xn_ref, g_ref, w_ref, bin_ref, cw_ref, cb_ref,
                    v_ref, g1_ref, g2_ref, *, tm, n_tiles):
    xne = _normed_with_halo(xm_ref, xp_ref, xn_ref, g_ref[...])
    valid = _halo_valid(tm, n_tiles)
    for part, ref in enumerate((v_ref, g1_ref, g2_ref)):
        cols = slice(part * D_MODEL, (part + 1) * D_MODEL)
        p = jnp.where(valid, _dot(xne, w_ref[:, cols]) + bin_ref[:, cols], 0.0)
        ref[0] = (_conv3(p, cw_ref[:, cols], tm) + cb_ref[:, cols]).astype(BF16)


def hy_project(x, g_pre, w_in, b_in, conv_w, conv_b):
    bsz, seq, _ = x.shape
    tm = PROJ_TM
    nt = seq // tm
    out = jax.ShapeDtypeStruct((bsz, seq, D_MODEL), BF16)
    spec = pl.BlockSpec((1, tm, D_MODEL), lambda b, i: (b, i, 0))
    return pl.pallas_call(
        functools.partial(_hy_proj_kernel, tm=tm, n_tiles=nt),
        grid=(bsz, nt),
        in_specs=_halo_specs(tm, seq) + [
            _const_spec((1, D_MODEL)), _const_spec((D_MODEL, 3 * D_MODEL)), _const_spec((1, 3 * D_MODEL)),
            _const_spec((3, 3 * D_MODEL)), _const_spec((1, 3 * D_MODEL))],
        out_specs=[spec, spec, spec],
        out_shape=[out, out, out],
        compiler_params=_params("parallel", "parallel"),
        name="hy_project",
    )(x, x, x, g_pre, w_in, b_in, conv_w, conv_b)


def _dot_f32(a, b):
    return jnp.dot(a, b, preferred_element_type=F32, precision=lax.Precision.HIGHEST)


def _hy_filter_kernel(f_ref, w1t_ref, w1c_ref, w1s_ref, b1_ref, fr1_ref, w2_ref, b2_ref, fr2_ref, w3_ref,
                      delta_ref, h_ref, *, tl, seq):
    n = (lax.broadcasted_iota(jnp.int32, (tl, 1), 0) + pl.program_id(0) * tl).astype(F32)
    t = n / (seq - 1.0)
    arg = (2.0 * math.pi / seq) * n * f_ref[...]
    pre = (t * w1t_ref[...] + _dot_f32(jnp.cos(arg), w1c_ref[...]) + _dot_f32(-jnp.sin(arg), w1s_ref[...])
           + b1_ref[...])
    h1 = jnp.sin(fr1_ref[...] * pre)
    h2 = jnp.sin(fr2_ref[...] * (_dot_f32(h1, w2_ref[...]) + b2_ref[...]))
    h3 = _dot_f32(h2, w3_ref[...])
    window = jnp.exp(-t * delta_ref[...])
    for s in range(4):
        hs = h3[:, s * D_MODEL:(s + 1) * D_MODEL] * window
        if s >= 2:
            hs = jnp.where(n == 0.0, 0.0, hs)
        h_ref[s] = hs


def hy_filters(seq, f_w1, f_b1, f_fr1, f_w2, f_b2, f_fr2, f_w3):
    bands = (C_EMB - 1) // 2
    fw = C_FILTER_WIDTH
    f = np.zeros((1, LANE), np.float32)
    f[0, :bands] = np.linspace(1e-4, bands - 1, bands, dtype=np.float32)
    deltas = np.abs(np.linspace(C_MIN_DECAY, C_MAX_DECAY, D_MODEL, dtype=np.float32))[None, :]
    pad = ((0, LANE - bands), (0, 0))
    w1c = jnp.pad(f_w1[1:1 + bands], pad)
    w1s = jnp.pad(f_w1[1 + bands:], pad)
    tl = FILTER_TL
    row = lambda a: a.reshape(1, -1)
    args = (jnp.asarray(f), f_w1[0:1], w1c, w1s, row(f_b1), row(f_fr1), f_w2, row(f_b2), row(f_fr2), f_w3,
            jnp.asarray(deltas))
    return pl.pallas_call(
        functools.partial(_hy_filter_kernel, tl=tl, seq=seq),
        grid=(seq // tl,),
        in_specs=[_const_spec(a.shape) for a in args],
        out_specs=pl.BlockSpec((4, tl, D_MODEL), lambda i: (0, i, 0)),
        out_shape=jax.ShapeDtypeStruct((4, seq, D_MODEL), F32),
        compiler_params=_params("parallel"),
        name="hy_filters",
    )(*args)


def _dft1_kernel(u_ref, m1_ref, y_ref):
    p, half, lw = u_ref.shape[1:]
    y_ref[0] = _dot(m1_ref[...], u_ref[0].reshape(p * half, lw).astype(BF16)).astype(BF16)


def dft_stage1(u4, m1):
    g, p, half, width = u4.shape
    rows = m1.shape[0]
    lw = HY_LANES
    return pl.pallas_call(
        _dft1_kernel,
        grid=(g, width // lw),
        in_specs=[pl.BlockSpec((1, p, half, lw), lambda b, j: (b, 0, 0, j)), _const_spec(m1.shape)],
        out_specs=pl.BlockSpec((1, rows, lw), lambda b, j: (b, 0, j)),
        out_shape=jax.ShapeDtypeStruct((g, rows, width), BF16),
        compiler_params=_params("parallel", "parallel"),
        name="dft_stage1",
    )(u4, m1)


def _slab(ref):
    n2, cb = ref.shape[3:]
    return ref[0, :, 0].reshape(2 * n2, cb)


def _filter_spec_kernel(y_ref, m2f_ref, x_ref):
    n2, cb = y_ref.shape[3:]
    x_ref[0, :, 0] = _dot(m2f_ref[0], _slab(y_ref)).reshape(2, n2, cb)


def filter_spectrum(y5, m2f):
    s, _, n1, n2, c = y5.shape
    cb = HY_CB
    blk = pl.BlockSpec((1, 2, 1, n2, cb), lambda k, j, b: (b, 0, k, 0, j))
    return pl.pallas_call(
        _filter_spec_kernel,
        grid=(n1, c // cb, s),
        in_specs=[blk, pl.BlockSpec((1, 2 * n2, 2 * n2), lambda k, j, b: (k, 0, 0))],
        out_specs=blk,
        out_shape=jax.ShapeDtypeStruct(y5.shape, F32),
        compiler_params=_params("parallel", "parallel", "parallel"),
        name="filter_spectrum",
    )(y5, m2f)


def _conv_freq_kernel(y_ref, m2f_ref, m2i_ref, hf_ref, hb_ref, z_ref):
    n2, cb = y_ref.shape[3:]
    x = _dot(m2f_ref[0], _slab(y_ref))
    xr, xi = x[:n2], x[n2:]
    hr = hf_ref[0, 0, 0] + hb_ref[0, 0, 0]
    hi = hf_ref[0, 1, 0] - hb_ref[0, 1, 0]
    prod = jnp.concatenate([xr * hr - xi * hi, xr * hi + xi * hr], axis=0).astype(BF16)
    z_ref[0, :, 0] = _dot(m2i_ref[0], prod).reshape(2, n2, cb).astype(BF16)


def conv_freq(y5, m2f, m2i, hspec, order):
    g, _, n1, n2, c = y5.shape
    cb = HY_CB
    blk = pl.BlockSpec((1, 2, 1, n2, cb), lambda k, j, b: (b, 0, k, 0, j))
    tab = pl.BlockSpec((1, 2 * n2, 2 * n2), lambda k, j, b: (k, 0, 0))
    hblk = lambda s: pl.BlockSpec((1, 2, 1, n2, cb), lambda k, j, b: (s, 0, k, 0, j))
    return pl.pallas_call(
        _conv_freq_kernel,
        grid=(n1, c // cb, g),
        in_specs=[blk, tab, tab, hblk(order), hblk(2 + order)],
        out_specs=blk,
        out_shape=jax.ShapeDtypeStruct(y5.shape, BF16),
        compiler_params=_params("parallel", "parallel", "parallel"),
        name="conv_freq",
    )(y5, m2f, m2i, hspec, hspec)


def _conv_time_kernel(z_ref, u_ref, gate_ref, skip_ref, m3_ref, *rest, chain):
    half, lw = u_ref.shape[2:]
    y = _dot(m3_ref[...], z_ref[0])
    u = u_ref[0].reshape(2 * half, lw).astype(F32)
    out = (gate_ref[0].reshape(2 * half, lw).astype(F32) * (y + skip_ref[...] * u)).astype(BF16)
    if chain:
        m1_ref, o_ref, y_ref = rest
        y_ref[0] = _dot(m1_ref[...], out).astype(BF16)
    else:
        (o_ref,) = rest
    o_ref[0] = out.reshape(2, half, lw)


def conv_time(z3, u4, gate4, skip, m3, m1=None):
    g, _, half, width = u4.shape
    rows = z3.shape[1]
    lw = HY_LANES
    chain = m1 is not None
    sig = pl.BlockSpec((1, 2, half, lw), lambda b, j: (b, 0, 0, j))
    spec = pl.BlockSpec((1, rows, lw), lambda b, j: (b, 0, j))
    in_specs = [spec, sig, sig, _const_spec((1, lw)), _const_spec(m3.shape)]
    out_specs = [sig]
    out_shape = [jax.ShapeDtypeStruct(u4.shape, BF16)]
    args = [z3, u4, gate4, skip, m3]
    if chain:
        in_specs.append(_const_spec(m1.shape))
        out_specs.append(spec)
        out_shape.append(jax.ShapeDtypeStruct(z3.shape, BF16))
        args.append(m1)
    res = pl.pallas_call(
        functools.partial(_conv_time_kernel, chain=chain),
        grid=(g, width // lw),
        in_specs=in_specs,
        out_specs=out_specs,
        out_shape=out_shape,
        compiler_params=_params("parallel", "parallel"),
        name="conv_time_chain" if chain else "conv_time",
    )(*args)
    return res if chain else res[0]


def _hy_tail(z_ref, x_ref, wo_ref, bo_ref, g_ref):
    hout = _dot(z_ref[...], wo_ref[...]) + bo_ref[...]
    return x_ref[...] + _rms(hout, g_ref[...])


def hyena_mixer_block(x, g_pre, g_post, w_in, b_in, conv_w, conv_b, f_w1, f_b1, f_fr1, f_w2, f_b2, f_fr2, f_w3,
                      bias_d, w_out, b_out, ffn_args):
    bsz, seq, _ = x.shape
    n1, m1, m1_real, m3, m2f, m2i = _dft_tables(seq)
    n2 = DFT_N2
    width = n2 * D_MODEL
    pairs = bsz // 2
    sig4 = lambda a: a.reshape(pairs, 2, n1 // 2, width)
    five = lambda a: a.reshape(a.shape[0], 2, n1, n2, D_MODEL)

    filt = hy_filters(seq, f_w1, f_b1, f_fr1, f_w2, f_b2, f_fr2, f_w3)
    hspec = filter_spectrum(five(dft_stage1(filt.reshape(4, 1, n1 // 2, width), m1_real)), m2f)

    v, g1, g2 = hy_project(x, g_pre.reshape(1, -1), w_in.astype(BF16), b_in.reshape(1, -1), conv_w,
                           conv_b.reshape(1, -1))
    v4, g14, g24 = sig4(v), sig4(g1), sig4(g2)
    y = dft_stage1(v4, m1)
    z = conv_freq(five(y), m2f, m2i, hspec, 0).reshape(pairs, 2 * n1, width)
    skip = jnp.tile(bias_d, (1, HY_LANES // D_MODEL))
    z1, y = conv_time(z, v4, g14, skip[0:1], m3, m1)
    z = conv_freq(five(y), m2f, m2i, hspec, 1).reshape(pairs, 2 * n1, width)
    z2 = conv_time(z, z1, g24, skip[1:2], m3)
    t = bsz * seq
    args = (z2.reshape(t, D_MODEL), x.reshape(t, D_MODEL), w_out.astype(BF16), b_out.reshape(1, -1),
            g_post.reshape(1, -1))
    specs = [_row_spec(D_MODEL), _row_spec(D_MODEL), _const_spec((D_MODEL, D_MODEL)), _const_spec((1, D_MODEL)),
             _const_spec((1, D_MODEL))]
    return tail_ffn_block(_hy_tail, args, specs, ffn_args, "hy_tail_ffn").reshape(bsz, seq, D_MODEL)


def kernel(x_prompt, x_sample, norm_g, ffn_w_gate, ffn_w_up, ffn_w_down, a_w_in, a_conv_w, a_a_log, a_dt_bias,
           a_norm_w, a_w_out, b_w_in, b_decay_logit, b_gn_w, b_w_out, c_w_in, c_b_in, c_conv_w, c_conv_b,
           c_f_w1, c_f_b1, c_f_freq1, c_f_w2, c_f_b2, c_f_freq2, c_f_w3, c_bias_d, c_w_out, c_b_out):
    depth = norm_g.shape[0]
    wg = ffn_w_gate.astype(BF16)
    wu = ffn_w_up.astype(BF16)
    wd = ffn_w_down.astype(BF16)

    def ffn_args(i, which):
        return (norm_g[i, 4 * which:4 * which + 2], wg[i, which], wu[i, which], wd[i, which])

    def mixer_then_ffn(x, i):
        kind, j = i % 3, i // 3
        g_pre, g_post = norm_g[i, 2], norm_g[i, 3]
        ffn = ffn_args(i, 1)
        if kind == 0:
            return gdn_mixer_block(x, g_pre, g_post, a_w_in[j], a_conv_w[j], a_a_log[j], a_dt_bias[j],
                                   a_norm_w[j], a_w_out[j], ffn)
        if kind == 1:
            return retnet_mixer_block(x, g_pre, g_post, b_w_in[j], b_decay_logit[j], b_gn_w[j], b_w_out[j], ffn)
        return hyena_mixer_block(x, g_pre, g_post, c_w_in[j], c_b_in[j], c_conv_w[j], c_conv_b[j], c_f_w1[j],
                                 c_f_b1[j], c_f_freq1[j], c_f_w2[j], c_f_b2[j], c_f_freq2[j], c_f_w3[j],
                                 c_bias_d[j], c_w_out[j], c_b_out[j], ffn)

    def trunk(x):
        bsz, seq, _ = x.shape
        for i in range(depth):
            x = ffn_block(x.reshape(bsz * seq, D_MODEL), *ffn_args(i, 0)).reshape(bsz, seq, D_MODEL)
            x = mixer_then_ffn(x, i)
        return x

    return (trunk(x_prompt), trunk(x_sample))
```

```python
import functools
import math

import numpy as np
import jax
import jax.numpy as jnp
from jax import lax
from jax.experimental import pallas as pl
from jax.experimental.pallas import tpu as pltpu

F32 = jnp.float32
BF16 = jnp.bfloat16
EPS = 1e-6

D_MODEL = 1024
D_FF = 2816

A_HEADS, A_DK, A_DV, A_CHUNK = 8, 128, 128, 64
A_QK = A_HEADS * A_DK
A_V = A_HEADS * A_DV
B_HEADS, B_DK, B_DV = 4, 256, 512
ROPE_BASE = 10000.0
C_EMB, C_FILTER_WIDTH = 33, 64
C_TARGET, C_SHORT_DECAY_PCT, C_LONG_DECAY_PCT = 1e-2, 0.3, 1.5
C_MIN_DECAY = math.log(C_TARGET) / C_LONG_DECAY_PCT
C_MAX_DECAY = math.log(C_TARGET) / C_SHORT_DECAY_PCT

V7X_VMEM_BYTES = 64 * 1024 * 1024
VMEM_LIMIT = V7X_VMEM_BYTES - 8 * 1024 * 1024
SUBLANE = 8
LANE = 128
MXU_DIM = 256

HALO = SUBLANE


def _params(*sem):
    return pltpu.CompilerParams(dimension_semantics=sem, vmem_limit_bytes=VMEM_LIMIT)


def _const_spec(shape):
    nd = len(shape)
    return pl.BlockSpec(shape, lambda *_: (0,) * nd, pipeline_mode=pl.Buffered(1))


def _rms(x, g):
    return x * lax.rsqrt(jnp.mean(x * x, axis=-1, keepdims=True) + EPS) * g


def _silu(x):
    return x * jax.nn.sigmoid(x)


def _dot(a, b):
    return jnp.dot(a, b, preferred_element_type=F32)


def _dot_nt(a, b):
    return lax.dot_general(a, b, (((1,), (1,)), ((), ())), preferred_element_type=F32)


def _dot_tn(a, b):
    return lax.dot_general(a, b, (((0,), (0,)), ((), ())), preferred_element_type=F32)


FFN_TM = 512
FFN_CHUNKS = 2


def _ffn_value(x, g, wg_ref, wu_ref, wd_ref):
    xn = _rms(x, g[0:1]).astype(BF16)
    fc = D_FF // FFN_CHUNKS
    y = None
    for c in range(FFN_CHUNKS):
        hg = _dot(xn, wg_ref[:, c * fc:(c + 1) * fc])
        hu = _dot(xn, wu_ref[:, c * fc:(c + 1) * fc])
        act = (_silu(hg) * hu).astype(BF16)
        part = _dot(act, wd_ref[c * fc:(c + 1) * fc, :])
        y = part if y is None else y + part
    return x + 0.5 * _rms(y, g[1:2])


def _ffn_kernel(x_ref, g_ref, wg_ref, wu_ref, wd_ref, o_ref):
    o_ref[...] = _ffn_value(x_ref[...], g_ref[...], wg_ref, wu_ref, wd_ref)


def _tail_ffn_kernel(*refs, tail, n_tail):
    g_ref, wg_ref, wu_ref, wd_ref, o_ref = refs[n_tail:]
    o_ref[...] = _ffn_value(tail(*refs[:n_tail]), g_ref[...], wg_ref, wu_ref, wd_ref)


def _ffn_specs():
    return [_const_spec((2, D_MODEL)), _const_spec((D_MODEL, D_FF)), _const_spec((D_MODEL, D_FF)),
            _const_spec((D_FF, D_MODEL))]


def tail_ffn_block(tail, tail_args, tail_specs, ffn_args, name):
    t = tail_args[0].shape[0]
    tm = FFN_TM
    return pl.pallas_call(
        functools.partial(_tail_ffn_kernel, tail=tail, n_tail=len(tail_args)),
        grid=(t // tm,),
        in_specs=list(tail_specs) + _ffn_specs(),
        out_specs=pl.BlockSpec((tm, D_MODEL), lambda i: (i, 0)),
        out_shape=jax.ShapeDtypeStruct((t, D_MODEL), F32),
        compiler_params=_params("parallel"),
        name=name,
    )(*tail_args, *ffn_args)


def _row_spec(cols):
    return pl.BlockSpec((FFN_TM, cols), lambda i: (i, 0))


def ffn_block(x2, g_pair, wg, wu, wd):
    t = x2.shape[0]
    tm = FFN_TM
    return pl.pallas_call(
        _ffn_kernel,
        grid=(t // tm,),
        in_specs=[_row_spec(D_MODEL)] + _ffn_specs(),
        out_specs=_row_spec(D_MODEL),
        out_shape=jax.ShapeDtypeStruct((t, D_MODEL), F32),
        compiler_params=_params("parallel"),
        name="ffn_block",
    )(x2, g_pair, wg, wu, wd)


PROJ_TM = 512


def _halo_specs(tm, seq):
    per = tm // HALO
    last = seq // HALO - 1
    return [
        pl.BlockSpec((1, tm, D_MODEL), lambda b, i: (b, i, 0)),
        pl.BlockSpec((1, HALO, D_MODEL), lambda b, i: (b, jnp.maximum(i * per - 1, 0), 0)),
        pl.BlockSpec((1, HALO, D_MODEL), lambda b, i: (b, jnp.minimum((i + 1) * per, last), 0)),
    ]


def _normed_with_halo(xm_ref, xp_ref, xn_ref, g):
    xe = jnp.concatenate([xp_ref[0], xm_ref[0], xn_ref[0]], axis=0)
    return _rms(xe, g).astype(BF16)


def _halo_valid(tm, n_tiles):
    i = pl.program_id(1)
    row = lax.broadcasted_iota(jnp.int32, (tm + 2 * HALO, 1), 0)
    before = jnp.logical_and(row < HALO, i == 0)
    after = jnp.logical_and(row >= tm + HALO, i == n_tiles - 1)
    return jnp.logical_not(jnp.logical_or(before, after))


def _conv3(p, cw, tm):
    rows = p.shape[0]
    prev = pltpu.roll(p, 1, 0)[HALO:HALO + tm]
    nxt = pltpu.roll(p, rows - 1, 0)[HALO:HALO + tm]
    return prev * cw[0:1] + p[HALO:HALO + tm] * cw[1:2] + nxt * cw[2:3]


GATE_LANES = LANE


def _chunk_scan(val, pos, size, axis, reverse):
    s = 1
    while s < A_CHUNK:
        if reverse:
            shifted = pltpu.roll(val, size - s, axis)
            val = val + jnp.where(pos < A_CHUNK - s, shifted, 0.0)
        else:
            shifted = pltpu.roll(val, s, axis)
            val = val + jnp.where(pos >= s, shifted, 0.0)
        s *= 2
    return val


def _gdn_proj_kernel(xm_ref, xp_ref, xn_ref, g_ref, wqkv_ref, wz_ref, wab_ref, cw_ref, alog_ref, dtb_ref,
                     q_ref, k_ref, v_ref, z_ref, gcol_ref, grow_ref, *, tm, n_tiles):
    xne = _normed_with_halo(xm_ref, xp_ref, xn_ref, g_ref[...])
    valid = _halo_valid(tm, n_tiles)
    outs = (q_ref, k_ref, v_ref)
    for part in range(3):
        cols = slice(part * A_QK, (part + 1) * A_QK)
        p = jnp.where(valid, _dot(xne, wqkv_ref[:, cols]), 0.0)
        a = _silu(_conv3(p, cw_ref[:, cols], tm))
        if part == 2:
            v_ref[0] = a.astype(BF16)
            continue
        scale = A_DK ** -0.5 if part == 0 else 1.0
        for h in range(A_HEADS):
            ah = a[:, h * A_DK:(h + 1) * A_DK]
            inv = lax.rsqrt(jnp.sum(ah * ah, axis=-1, keepdims=True) + EPS) * scale
            outs[part][0, :, h * A_DK:(h + 1) * A_DK] = (ah * inv).astype(BF16)
    xn_main = xne[HALO:HALO + tm]
    z_ref[0] = _dot(xn_main, wz_ref[...]).astype(BF16)
    ab = _dot(xn_main, wab_ref[...])
    col = lax.broadcasted_iota(jnp.int32, (1, GATE_LANES), 1)
    is_decay = jnp.logical_and(col % 16 < 8, col < 32)
    sp = ab + dtb_ref[...]
    softplus = jnp.maximum(sp, 0.0) + jnp.log(1.0 + jnp.exp(-jnp.abs(sp)))
    val = jnp.where(is_decay, -jnp.exp(alog_ref[...]) * softplus, jax.nn.sigmoid(ab))
    pos = lax.broadcasted_iota(jnp.int32, (tm, 1), 0) % A_CHUNK
    fwd = _chunk_scan(val, pos, tm, 0, False)
    bwd = _chunk_scan(val, pos, tm, 0, True)
    gate = jnp.where(col < 8, fwd, jnp.where(jnp.logical_and(col >= 16, col < 24), bwd, val))
    gcol_ref[0] = gate
    grow_ref[0] = gate.T[0:4 * A_HEADS, :]


def gdn_project(x, g_pre, wqkv, wz, wab, conv_w, alog_row, dtb_row):
    bsz, seq, _ = x.shape
    tm = PROJ_TM
    nt = seq // tm
    act = lambda c: jax.ShapeDtypeStruct((bsz, seq, c), BF16)
    act_spec = lambda c: pl.BlockSpec((1, tm, c), lambda b, i: (b, i, 0))
    return pl.pallas_call(
        functools.partial(_gdn_proj_kernel, tm=tm, n_tiles=nt),
        grid=(bsz, nt),
        in_specs=_halo_specs(tm, seq) + [
            _const_spec((1, D_MODEL)),
            _const_spec((D_MODEL, 3 * A_QK)),
            _const_spec((D_MODEL, A_V)),
            _const_spec((D_MODEL, GATE_LANES)),
            _const_spec((3, 3 * A_QK)),
            _const_spec((1, GATE_LANES)),
            _const_spec((1, GATE_LANES)),
        ],
        out_specs=[act_spec(A_QK), act_spec(A_QK), act_spec(A_V), act_spec(A_V),
                   pl.BlockSpec((1, tm, GATE_LANES), lambda b, i: (b, i, 0)),
                   pl.BlockSpec((1, 4 * A_HEADS, tm), lambda b, i: (b, 0, i))],
        out_shape=[act(A_QK), act(A_QK), act(A_V), act(A_V),
                   jax.ShapeDtypeStruct((bsz, seq, GATE_LANES), F32),
                   jax.ShapeDtypeStruct((bsz, 4 * A_HEADS, seq), F32)],
        compiler_params=_params("parallel", "parallel"),
        name="gdn_project",
    )(x, x, x, g_pre, wqkv, wz, wab, conv_w, alog_row, dtb_row)


GDN_HPB = 8
GDN_BLK = 256


def _gdn_direction(q_ref, k_ref, v_ref, gcol_ref, grow_ref, s_ref, o_ref, hh, head, reverse):
    n = GDN_BLK
    hcols = slice(hh * A_DK, (hh + 1) * A_DK)
    q = q_ref[0, :, hcols].astype(F32)
    k = k_ref[0, :, hcols].astype(F32)
    v = v_ref[0, :, hcols].astype(F32)
    d = 1 if reverse else 0
    gcol = gcol_ref[0]
    lane = lax.broadcasted_iota(jnp.int32, (1, GATE_LANES), 1)
    pick = lambda c: jnp.sum(jnp.where(lane == c, gcol, 0.0), axis=-1, keepdims=True)
    gc = pick(d * 16 + head)
    beta = pick(d * 16 + 8 + head)
    gr = grow_ref[0, pl.ds(d * 16 + head, 1), :]

    ri = lax.broadcasted_iota(jnp.int32, (n, n), 0)
    ci = lax.broadcasted_iota(jnp.int32, (n, n), 1)
    same = (ri // A_CHUNK) == (ci // A_CHUNK)
    incl = jnp.logical_and(same, (ri <= ci) if reverse else (ri >= ci))
    strict = jnp.logical_and(incl, ri != ci)
    decay = jnp.where(incl, jnp.exp(jnp.where(incl, gc - gr, 0.0)), 0.0)

    kb = k * beta
    kb16 = kb.astype(BF16)
    k16 = k.astype(BF16)
    kk = _dot_nt(kb16, k16)
    qk = _dot_nt(q.astype(BF16), k16)
    yield
    a = jnp.where(strict, kk * decay, 0.0)
    attn16 = (qk * decay).astype(BF16)

    n_chunks = n // A_CHUNK
    side_by_side = lambda m: sum(m[c * A_CHUNK:(c + 1) * A_CHUNK] for c in range(n_chunks))
    block_diag16 = lambda m: jnp.where(same, jnp.concatenate([m] * n_chunks, axis=0), 0.0).astype(BF16)
    a_side = side_by_side(a)
    eye_side = (lax.broadcasted_iota(jnp.int32, (A_CHUNK, n), 0)
                == lax.broadcasted_iota(jnp.int32, (A_CHUNK, n), 1) % A_CHUNK)
    t_side = jnp.where(eye_side, 1.0, 0.0) - a_side
    a_pow = _dot(a_side.astype(BF16), a.astype(BF16))
    yield
    s = 2
    while s < A_CHUNK:
        ap16 = block_diag16(a_pow)
        upd = _dot(t_side.astype(BF16), ap16)
        s *= 2
        if s < A_CHUNK:
            a_pow = _dot(a_pow.astype(BF16), ap16)
        yield
        t_side = t_side + upd

    eg = jnp.exp(gc)
    uw = _dot(block_diag16(t_side), jnp.concatenate([v * beta, kb * eg], axis=1).astype(BF16))
    yield
    u = uw[:, :A_DV]
    w = uw[:, A_DV:]
    qd = q * eg

    state = s_ref[hh]
    n_chunks = n // A_CHUNK
    order = range(n_chunks - 1, -1, -1) if reverse else range(n_chunks)
    for c in order:
        r0 = c * A_CHUNK
        rows = slice(r0, r0 + A_CHUNK)
        last = r0 if reverse else r0 + A_CHUNK - 1
        g_last = gc[last:last + 1, :]
        s16 = state.astype(BF16)
        wq = _dot(jnp.concatenate([w[rows], qd[rows]], axis=0).astype(BF16), s16)
        yield
        v_new = u[rows] - wq[:A_CHUNK]
        vn16 = v_new.astype(BF16)
        intra = _dot(attn16[rows, r0:r0 + A_CHUNK], vn16)
        kd = k[rows] * jnp.exp(g_last - gc[rows])
        s_add = _dot_tn(kd.astype(BF16), vn16)
        yield
        o_ref[0, rows, hcols] = (wq[A_CHUNK:] + intra).astype(BF16)
        state = state * jnp.exp(g_last) + s_add
    s_ref[hh] = state


def _gdn_core_kernel(qf_ref, kf_ref, vf_ref, gcf_ref, grf_ref, qb_ref, kb_ref, vb_ref, gcb_ref, grb_ref,
                     of_ref, ob_ref, sf_ref, sb_ref):
    @pl.when(pl.program_id(2) == 0)
    def _():
        sf_ref[...] = jnp.zeros_like(sf_ref)
        sb_ref[...] = jnp.zeros_like(sb_ref)

    chains = []
    for hh in range(GDN_HPB):
        head = hh if GDN_HPB == A_HEADS else pl.program_id(1) * GDN_HPB + hh
        chains.append(_gdn_direction(qf_ref, kf_ref, vf_ref, gcf_ref, grf_ref, sf_ref, of_ref, hh, head, False))
        chains.append(_gdn_direction(qb_ref, kb_ref, vb_ref, gcb_ref, grb_ref, sb_ref, ob_ref, hh, head, True))
    while chains:
        chains = [c for c in chains if next(c, True) is None]


def gdn_core(q, k, v, gcol, grow):
    bsz, seq, _ = q.shape
    n = GDN_BLK
    nb = seq // n
    w = GDN_HPB * A_DK
    fwd = lambda b, h, j: (b, j, h)
    bwd = lambda b, h, j: (b, nb - 1 - j, h)
    specs = []
    for im in (fwd, bwd):
        specs += [pl.BlockSpec((1, n, w), im), pl.BlockSpec((1, n, w), im), pl.BlockSpec((1, n, w), im),
                  pl.BlockSpec((1, n, GATE_LANES), (lambda im: lambda b, h, j: (b, im(b, h, j)[1], 0))(im)),
                  pl.BlockSpec((1, 4 * A_HEADS, n), (lambda im: lambda b, h, j: (b, 0, im(b, h, j)[1]))(im))]
    out = jax.ShapeDtypeStruct((bsz, seq, A_V), BF16)
    return pl.pallas_call(
        _gdn_core_kernel,
        grid=(bsz, A_HEADS // GDN_HPB, nb),
        in_specs=specs,
        out_specs=[pl.BlockSpec((1, n, w), fwd), pl.BlockSpec((1, n, w), bwd)],
        out_shape=[out, out],
        scratch_shapes=[pltpu.VMEM((GDN_HPB, A_DK, A_DV), F32), pltpu.VMEM((GDN_HPB, A_DK, A_DV), F32)],
        compiler_params=_params("parallel", "parallel", "arbitrary"),
        name="gdn_core",
    )(q, k, v, gcol, grow, q, k, v, gcol, grow)


def _gdn_tail(of_ref, ob_ref, z_ref, x_ref, nw_ref, wo_ref, g_ref):
    nw = nw_ref[...]
    parts = []
    for h in range(A_HEADS):
        cols = slice(h * A_DV, (h + 1) * A_DV)
        o = of_ref[:, cols].astype(F32) + ob_ref[:, cols].astype(F32)
        o = o * lax.rsqrt(jnp.mean(o * o, axis=-1, keepdims=True) + EPS) * nw
        parts.append((o * _silu(z_ref[:, cols].astype(F32))).astype(BF16))
    hout = _dot(jnp.concatenate(parts, axis=1), wo_ref[...])
    return x_ref[...] + _rms(hout, g_ref[...])


def gdn_mixer_block(x, g_pre, g_post, w_in, conv_w, a_log, dt_bias, norm_w, w_out, ffn_args):
    bsz, seq, _ = x.shape
    n_qkv = 2 * A_QK + A_V
    wqkv = w_in[:, :n_qkv].astype(BF16)
    wz = w_in[:, n_qkv:n_qkv + A_V].astype(BF16)
    pad = GATE_LANES - 4 * A_HEADS
    wab = jnp.pad(w_in[:, n_qkv + A_V:], ((0, 0), (0, pad))).astype(BF16)
    zeros = jnp.zeros((2, A_HEADS), F32)
    alog_row = jnp.pad(jnp.stack([a_log, zeros], axis=1).reshape(1, 4 * A_HEADS), ((0, 0), (0, pad)))
    dtb_row = jnp.pad(jnp.stack([dt_bias, zeros], axis=1).reshape(1, 4 * A_HEADS), ((0, 0), (0, pad)))
    q, k, v, z, gcol, grow = gdn_project(x, g_pre.reshape(1, -1), wqkv, wz, wab, conv_w, alog_row, dtb_row)
    o_f, o_b = gdn_core(q, k, v, gcol, grow)
    t = bsz * seq
    args = (o_f.reshape(t, A_V), o_b.reshape(t, A_V), z.reshape(t, A_V), x.reshape(t, D_MODEL),
            norm_w.reshape(1, A_DV), w_out.astype(BF16), g_post.reshape(1, -1))
    specs = [_row_spec(A_V), _row_spec(A_V), _row_spec(A_V), _row_spec(D_MODEL), _const_spec((1, A_DV)),
             _const_spec((A_V, D_MODEL)), _const_spec((1, D_MODEL))]
    return tail_ffn_block(_gdn_tail, args, specs, ffn_args, "gdn_tail_ffn").reshape(bsz, seq, D_MODEL)


RET_BLK = 256
DECAY_ROWS = SUBLANE


def _log_gamma(dl_ref):
    return jnp.log1p(-jnp.exp2(dl_ref[...]))


def _ret_proj_kernel(x_ref, g_ref, w_ref, cos_ref, sin_ref, dl_ref,
                     q_ref, qf_ref, qb_ref, k_ref, kf_ref, kb_ref, v_ref, gate_ref, *, tm):
    xn = _rms(x_ref[0], g_ref[...]).astype(BF16)
    cos = cos_ref[...]
    sin = sin_ref[...]
    nq = B_HEADS * B_DK
    nv = B_HEADS * B_DV
    half = B_DK // 2
    n = RET_BLK
    log_gamma = _log_gamma(dl_ref)
    pos = (lax.broadcasted_iota(jnp.int32, (tm, half), 0) % n).astype(F32)
    for part in range(2):
        p = _dot(xn, w_ref[:, part * nq:(part + 1) * nq])
        for h in range(B_HEADS):
            lgf = log_gamma[0:1, h:h + 1]
            lgb = log_gamma[1:2, h:h + 1]
            x1 = p[:, h * B_DK:h * B_DK + half]
            x2 = p[:, h * B_DK + half:(h + 1) * B_DK]
            if part == 0:
                variants = ((q_ref, None), (qf_ref, jnp.exp(lgf * (pos + 1.0))), (qb_ref, jnp.exp(lgb * (n - pos))))
                scale = 1.0
            else:
                variants = ((k_ref, None), (kf_ref, jnp.exp(lgf * (n - 1.0 - pos))), (kb_ref, jnp.exp(lgb * pos)))
                scale = B_DK ** -0.5
            r1 = (x1 * cos - x2 * sin) * scale
            r2 = (x2 * cos + x1 * sin) * scale
            for ref, fac in variants:
                ref[0, :, h * B_DK:h * B_DK + half] = (r1 if fac is None else r1 * fac).astype(BF16)
                ref[0, :, h * B_DK + half:(h + 1) * B_DK] = (r2 if fac is None else r2 * fac).astype(BF16)
    v_ref[0] = _dot(xn, w_ref[:, 2 * nq:2 * nq + nv]).astype(BF16)
    gate_ref[0] = _dot(xn, w_ref[:, 2 * nq + nv:]).astype(BF16)


def ret_project(x, g_pre, w_in, cos, sin, decay_logit_tile):
    bsz, seq, _ = x.shape
    tm = PROJ_TM
    nq = B_HEADS * B_DK
    nv = B_HEADS * B_DV
    act = lambda c: jax.ShapeDtypeStruct((bsz, seq, c), BF16)
    act_spec = lambda c: pl.BlockSpec((1, tm, c), lambda b, i: (b, i, 0))
    tab_spec = pl.BlockSpec((tm, B_DK // 2), lambda b, i: (i, 0))
    return pl.pallas_call(
        functools.partial(_ret_proj_kernel, tm=tm),
        grid=(bsz, seq // tm),
        in_specs=[act_spec(D_MODEL), _const_spec((1, D_MODEL)), _const_spec((D_MODEL, 2 * nq + 2 * nv)),
                  tab_spec, tab_spec, _const_spec((DECAY_ROWS, LANE))],
        out_specs=[act_spec(nq)] * 6 + [act_spec(nv), act_spec(nv)],
        out_shape=[act(nq)] * 6 + [act(nv), act(nv)],
        compiler_params=_params("parallel", "parallel"),
        name="ret_project",
    )(x, g_pre, w_in, cos, sin, decay_logit_tile)


def _ret_forward(q_ref, k_ref, qx_ref, kz_ref, v_ref, o_ref, r_ref, mask_ref, h, g_blk):
    qcols = slice(h * B_DK, (h + 1) * B_DK)
    vcols = slice(h * B_DV, (h + 1) * B_DV)
    v = v_ref[0, :, vcols]
    scores = _dot_nt(q_ref[0, :, qcols], k_ref[0, :, qcols])
    r = r_ref[h]
    cross = _dot(qx_ref[0, :, qcols], r.astype(BF16))
    r_add = _dot_tn(kz_ref[0, :, qcols], v)
    yield
    inner = _dot((scores * mask_ref[h]).astype(BF16), v)
    r_ref[h] = r * g_blk + r_add
    yield
    o_ref[0, :, vcols] = (inner + cross).astype(BF16)


def _ret_backward(qx_ref, kz_ref, v_ref, o_ref, r_ref, h, g_blk):
    qcols = slice(h * B_DK, (h + 1) * B_DK)
    vcols = slice(h * B_DV, (h + 1) * B_DV)
    r = r_ref[h]
    cross = _dot(qx_ref[0, :, qcols], r.astype(BF16))
    r_add = _dot_tn(kz_ref[0, :, qcols], v_ref[0, :, vcols])
    yield
    o_ref[0, :, vcols] = cross.astype(BF16)
    r_ref[h] = r * g_blk + r_add


def _ret_core_kernel(dl_ref, q_ref, k_ref, qxf_ref, kzf_ref, vf_ref, qxb_ref, kzb_ref, vb_ref,
                     of_ref, ob_ref, rf_ref, rb_ref, mask_ref):
    n = RET_BLK
    log_gamma = _log_gamma(dl_ref)

    @pl.when(pl.program_id(1) == 0)
    def _():
        rf_ref[...] = jnp.zeros_like(rf_ref)
        rb_ref[...] = jnp.zeros_like(rb_ref)
        ri = lax.broadcasted_iota(jnp.int32, (n, n), 0)
        ci = lax.broadcasted_iota(jnp.int32, (n, n), 1)
        dist = (ri - ci).astype(F32)
        for h in range(B_HEADS):
            lgf = log_gamma[0:1, h:h + 1]
            lgb = log_gamma[1:2, h:h + 1]
            mask_ref[h] = (jnp.where(dist >= 0, jnp.exp(lgf * jnp.maximum(dist, 0.0)), 0.0)
                           + jnp.where(dist <= 0, jnp.exp(lgb * jnp.maximum(-dist, 0.0)), 0.0))

    g_blk = jnp.exp(log_gamma * float(n))
    chains = []
    for h in range(B_HEADS):
        chains.append(_ret_forward(q_ref, k_ref, qxf_ref, kzf_ref, vf_ref, of_ref, rf_ref, mask_ref, h,
                                   g_blk[0:1, h:h + 1]))
        chains.append(_ret_backward(qxb_ref, kzb_ref, vb_ref, ob_ref, rb_ref, h, g_blk[1:2, h:h + 1]))
    while chains:
        chains = [c for c in chains if next(c, True) is None]


def ret_core(decay_logit_tile, q, qxf, qxb, k, kzf, kzb, v):
    bsz, seq, _ = q.shape
    n = RET_BLK
    nb = seq // n
    nq = B_HEADS * B_DK
    nv = B_HEADS * B_DV
    fwd = lambda b, j: (b, j, 0)
    bwd = lambda b, j: (b, nb - 1 - j, 0)
    qk = lambda im: pl.BlockSpec((1, n, nq), im)
    val = lambda im: pl.BlockSpec((1, n, nv), im)
    out = jax.ShapeDtypeStruct((bsz, seq, nv), BF16)
    state = pltpu.VMEM((B_HEADS, B_DK, B_DV), F32)
    return pl.pallas_call(
        _ret_core_kernel,
        grid=(bsz, nb),
        in_specs=[_const_spec((DECAY_ROWS, LANE)), qk(fwd), qk(fwd), qk(fwd), qk(fwd), val(fwd),
                  qk(bwd), qk(bwd), val(bwd)],
        out_specs=[val(fwd), val(bwd)],
        out_shape=[out, out],
        scratch_shapes=[state, state, pltpu.VMEM((B_HEADS, n, n), F32)],
        compiler_params=_params("parallel", "arbitrary"),
        name="ret_core",
    )(decay_logit_tile, q, k, qxf, kzf, v, qxb, kzb, v)


def _ret_tail(of_ref, ob_ref, gate_ref, x_ref, gn_ref, wo_ref, g_ref):
    parts = []
    for h in range(B_HEADS):
        cols = slice(h * B_DV, (h + 1) * B_DV)
        o = of_ref[:, cols].astype(F32) + ob_ref[:, cols].astype(F32)
        oc = o - jnp.mean(o, axis=-1, keepdims=True)
        on = oc * lax.rsqrt(jnp.mean(oc * oc, axis=-1, keepdims=True) + EPS) * gn_ref[:, cols]
        parts.append((_silu(gate_ref[:, cols].astype(F32)) * on).astype(BF16))
    hout = _dot(jnp.concatenate(parts, axis=1), wo_ref[...])
    return x_ref[...] + _rms(hout, g_ref[...])


def _rope_tables(seq):
    inv = ROPE_BASE ** (-np.arange(0, B_DK, 2, dtype=np.float64) / B_DK)
    ang = np.arange(seq, dtype=np.float64)[:, None] * inv[None, :]
    return jnp.asarray(np.cos(ang), F32), jnp.asarray(np.sin(ang), F32)


def retnet_mixer_block(x, g_pre, g_post, w_in, decay_logit, gn_w, w_out, ffn_args):
    bsz, seq, _ = x.shape
    cos, sin = _rope_tables(seq)
    dl = jnp.pad(decay_logit, ((0, DECAY_ROWS - 2), (0, LANE - B_HEADS)), constant_values=-1.0)
    q, qxf, qxb, k, kzf, kzb, v, gate = ret_project(x, g_pre.reshape(1, -1), w_in.astype(BF16), cos, sin, dl)
    o_f, o_b = ret_core(dl, q, qxf, qxb, k, kzf, kzb, v)
    t = bsz * seq
    nv = B_HEADS * B_DV
    args = (o_f.reshape(t, nv), o_b.reshape(t, nv), gate.reshape(t, nv), x.reshape(t, D_MODEL),
            gn_w.reshape(1, nv), w_out.astype(BF16), g_post.reshape(1, -1))
    specs = [_row_spec(nv), _row_spec(nv), _row_spec(nv), _row_spec(D_MODEL), _const_spec((1, nv)),
             _const_spec((nv, D_MODEL)), _const_spec((1, D_MODEL))]
    return tail_ffn_block(_ret_tail, args, specs, ffn_args, "ret_tail_ffn").reshape(bsz, seq, D_MODEL)


DFT_N2 = 128
HY_LANES = 4 * D_MODEL
HY_CB = D_MODEL
HY_PAIRS = 2
FILTER_TL = 256


def _dft_tables(seq):
    n = 2 * seq
    n2 = DFT_N2
    n1 = n // n2
    kk = np.arange(n1, dtype=np.float64)[:, None]
    nn = np.arange(n1 // 2, dtype=np.float64)[None, :]
    ang = 2.0 * np.pi * kk * nn / n1
    cc, ss = np.cos(ang), np.sin(ang)
    m1 = np.block([[cc, ss], [-ss, cc]])
    m1_real = np.concatenate([cc, -ss], axis=0)
    m3 = m1.T / n
    k1 = np.arange(n1, dtype=np.float64)[:, None, None]
    k2 = np.arange(n2, dtype=np.float64)[None, :, None]
    t2 = np.arange(n2, dtype=np.float64)[None, None, :]
    theta = 2.0 * np.pi * (t2 * k2 / n2 + t2 * k1 / n)
    c, s = np.cos(theta), np.sin(theta)
    m2f = np.concatenate([np.concatenate([c, s], axis=2), np.concatenate([-s, c], axis=2)], axis=1)
    m2i = np.transpose(m2f, (0, 2, 1))
    cast = lambda a: jnp.asarray(a, F32).astype(BF16)
    return n1, cast(m1), cast(m1_real), cast(m3), cast(m2f), cast(m2i)


def _hy_proj_kernel(xm_ref, xp_ref, xn_ref, g_ref, w_ref, bin_ref, cw_ref, cb_ref,
                    v_ref, g1_ref, g2_ref, *, tm, n_tiles):
    xne = _normed_with_halo(xm_ref, xp_ref, xn_ref, g_ref[...])
    valid = _halo_valid(tm, n_tiles)
    for part, ref in enumerate((v_ref, g1_ref, g2_ref)):
        cols = slice(part * D_MODEL, (part + 1) * D_MODEL)
        p = jnp.where(valid, _dot(xne, w_ref[:, cols]) + bin_ref[:, cols], 0.0)
        ref[0] = (_conv3(p, cw_ref[:, cols], tm) + cb_ref[:, cols]).astype(BF16)


def hy_project(x, g_pre, w_in, b_in, conv_w, conv_b):
    bsz, seq, _ = x.shape
    tm = PROJ_TM
    nt = seq // tm
    out = jax.ShapeDtypeStruct((bsz, seq, D_MODEL), BF16)
    spec = pl.BlockSpec((1, tm, D_MODEL), lambda b, i: (b, i, 0))
    return pl.pallas_call(
        functools.partial(_hy_proj_kernel, tm=tm, n_tiles=nt),
        grid=(bsz, nt),
        in_specs=_halo_specs(tm, seq) + [
            _const_spec((1, D_MODEL)), _const_spec((D_MODEL, 3 * D_MODEL)), _const_spec((1, 3 * D_MODEL)),
            _const_spec((3, 3 * D_MODEL)), _const_spec((1, 3 * D_MODEL))],
        out_specs=[spec, spec, spec],
        out_shape=[out, out, out],
        compiler_params=_params("parallel", "parallel"),
        name="hy_project",
    )(x, x, x, g_pre, w_in, b_in, conv_w, conv_b)


def _dot_f32(a, b):
    return jnp.dot(a, b, preferred_element_type=F32, precision=lax.Precision.HIGHEST)


def _hy_filter_kernel(f_ref, w1t_ref, w1c_ref, w1s_ref, b1_ref, fr1_ref, w2_ref, b2_ref, fr2_ref, w3_ref,
                      delta_ref, h_ref, *, tl, seq):
    n = (lax.broadcasted_iota(jnp.int32, (tl, 1), 0) + pl.program_id(0) * tl).astype(F32)
    t = n / (seq - 1.0)
    arg = (2.0 * math.pi / seq) * n * f_ref[...]
    pre = (t * w1t_ref[...] + _dot_f32(jnp.cos(arg), w1c_ref[...]) + _dot_f32(-jnp.sin(arg), w1s_ref[...])
           + b1_ref[...])
    h1 = jnp.sin(fr1_ref[...] * pre)
    h2 = jnp.sin(fr2_ref[...] * (_dot_f32(h1, w2_ref[...]) + b2_ref[...]))
    h3 = _dot_f32(h2, w3_ref[...])
    window = jnp.exp(-t * delta_ref[...])
    for s in range(4):
        hs = h3[:, s * D_MODEL:(s + 1) * D_MODEL] * window
        if s >= 2:
            hs = jnp.where(n == 0.0, 0.0, hs)
        h_ref[s] = hs


def hy_filters(seq, f_w1, f_b1, f_fr1, f_w2, f_b2, f_fr2, f_w3):
    bands = (C_EMB - 1) // 2
    fw = C_FILTER_WIDTH
    f = np.zeros((1, LANE), np.float32)
    f[0, :bands] = np.linspace(1e-4, bands - 1, bands, dtype=np.float32)
    deltas = np.abs(np.linspace(C_MIN_DECAY, C_MAX_DECAY, D_MODEL, dtype=np.float32))[None, :]
    pad = ((0, LANE - bands), (0, 0))
    w1c = jnp.pad(f_w1[1:1 + bands], pad)
    w1s = jnp.pad(f_w1[1 + bands:], pad)
    tl = FILTER_TL
    row = lambda a: a.reshape(1, -1)
    args = (jnp.asarray(f), f_w1[0:1], w1c, w1s, row(f_b1), row(f_fr1), f_w2, row(f_b2), row(f_fr2), f_w3,
            jnp.asarray(deltas))
    return pl.pallas_call(
        functools.partial(_hy_filter_kernel, tl=tl, seq=seq),
        grid=(seq // tl,),
        in_specs=[_const_spec(a.shape) for a in args],
        out_specs=pl.BlockSpec((4, tl, D_MODEL), lambda i: (0, i, 0)),
        out_shape=jax.ShapeDtypeStruct((4, seq, D_MODEL), F32),
        compiler_params=_params("parallel"),
        name="hy_filters",
    )(*args)


def _dft1_kernel(u_ref, m1_ref, y_ref):
    p, half, lw = u_ref.shape[1:]
    y_ref[0] = _dot(m1_ref[...], u_ref[0].reshape(p * half, lw).astype(BF16)).astype(BF16)


def dft_stage1(u4, m1):
    g, p, half, width = u4.shape
    rows = m1.shape[0]
    lw = HY_LANES
    return pl.pallas_call(
        _dft1_kernel,
        grid=(g, width // lw),
        in_specs=[pl.BlockSpec((1, p, half, lw), lambda b, j: (b, 0, 0, j)), _const_spec(m1.shape)],
        out_specs=pl.BlockSpec((1, rows, lw), lambda b, j: (b, 0, j)),
        out_shape=jax.ShapeDtypeStruct((g, rows, width), BF16),
        compiler_params=_params("parallel", "parallel"),
        name="dft_stage1",
    )(u4, m1)


def _filter_spec_kernel(y_ref, m2f_ref, h_ref):
    n2, cb = y_ref.shape[3:]
    x = [_dot(m2f_ref[0], y_ref[s, :, 0].reshape(2 * n2, cb)) for s in range(4)]
    for order in range(2):
        fwd, rev = x[order], x[2 + order]
        h_ref[order, 0, 0] = fwd[:n2] + rev[:n2]
        h_ref[order, 1, 0] = fwd[n2:] - rev[n2:]


def filter_spectrum(y5, m2f):
    s, _, n1, n2, c = y5.shape
    cb = HY_CB
    return pl.pallas_call(
        _filter_spec_kernel,
        grid=(n1, c // cb),
        in_specs=[pl.BlockSpec((s, 2, 1, n2, cb), lambda k, j: (0, 0, k, 0, j)),
                  pl.BlockSpec((1, 2 * n2, 2 * n2), lambda k, j: (k, 0, 0))],
        out_specs=pl.BlockSpec((2, 2, 1, n2, cb), lambda k, j: (0, 0, k, 0, j)),
        out_shape=jax.ShapeDtypeStruct((2, 2, n1, n2, c), F32),
        compiler_params=_params("parallel", "parallel"),
        name="filter_spectrum",
    )(y5, m2f)


def _conv_freq_kernel(y_ref, m2f_ref, m2i_ref, h_ref, z_ref):
    n2, cb = y_ref.shape[3:]
    hr = h_ref[0, 0, 0]
    hi = h_ref[0, 1, 0]
    for p in range(HY_PAIRS):
        x = _dot(m2f_ref[0], y_ref[p, :, 0].reshape(2 * n2, cb))
        xr, xi = x[:n2], x[n2:]
        prod = jnp.concatenate([xr * hr - xi * hi, xr * hi + xi * hr], axis=0).astype(BF16)
        z_ref[p, :, 0] = _dot(m2i_ref[0], prod).reshape(2, n2, cb).astype(BF16)


def conv_freq(y5, m2f, m2i, hspec, order):
    g, _, n1, n2, c = y5.shape
    cb = HY_CB
    blk = pl.BlockSpec((HY_PAIRS, 2, 1, n2, cb), lambda k, j, b: (b, 0, k, 0, j))
    tab = pl.BlockSpec((1, 2 * n2, 2 * n2), lambda k, j, b: (k, 0, 0))
    hblk = pl.BlockSpec((1, 2, 1, n2, cb), lambda k, j, b: (order, 0, k, 0, j))
    return pl.pallas_call(
        _conv_freq_kernel,
        grid=(n1, c // cb, g // HY_PAIRS),
        in_specs=[blk, tab, tab, hblk],
        out_specs=blk,
        out_shape=jax.ShapeDtypeStruct(y5.shape, BF16),
        compiler_params=_params("parallel", "parallel", "parallel"),
        name="conv_freq",
    )(y5, m2f, m2i, hspec)


def _conv_time_kernel(z_ref, u_ref, gate_ref, skip_ref, m3_ref, *rest, chain):
    half, lw = u_ref.shape[2:]
    y = _dot(m3_ref[...], z_ref[0])
    u = u_ref[0].reshape(2 * half, lw).astype(F32)
    out = (gate_ref[0].reshape(2 * half, lw).astype(F32) * (y + skip_ref[...] * u)).astype(BF16)
    if chain:
        m1_ref, o_ref, y_ref = rest
        y_ref[0] = _dot(m1_ref[...], out).astype(BF16)
    else:
        (o_ref,) = rest
    o_ref[0] = out.reshape(2, half, lw)


def conv_time(z3, u4, gate4, skip, m3, m1=None):
    g, _, half, width = u4.shape
    rows = z3.shape[1]
    lw = HY_LANES
    chain = m1 is not None
    sig = pl.BlockSpec((1, 2, half, lw), lambda b, j: (b, 0, 0, j))
    spec = pl.BlockSpec((1, rows, lw), lambda b, j: (b, 0, j))
    in_specs = [spec, sig, sig, _const_spec((1, lw)), _const_spec(m3.shape)]
    out_specs = [sig]
    out_shape = [jax.ShapeDtypeStruct(u4.shape, BF16)]
    args = [z3, u4, gate4, skip, m3]
    if chain:
        in_specs.append(_const_spec(m1.shape))
        out_specs.append(spec)
        out_shape.append(jax.ShapeDtypeStruct(z3.shape, BF16))
        args.append(m1)
    res = pl.pallas_call(
        functools.partial(_conv_time_kernel, chain=chain),
        grid=(g, width // lw),
        in_specs=in_specs,
        out_specs=out_specs,
        out_shape=out_shape,
        compiler_params=_params("parallel", "parallel"),
        name="conv_time_chain" if chain else "conv_time",
    )(*args)
    return res if chain else res[0]


def _hy_tail(z_ref, x_ref, wo_ref, bo_ref, g_ref):
    hout = _dot(z_ref[...], wo_ref[...]) + bo_ref[...]
    return x_ref[...] + _rms(hout, g_ref[...])


def hyena_mixer_block(x, g_pre, g_post, w_in, b_in, conv_w, conv_b, f_w1, f_b1, f_fr1, f_w2, f_b2, f_fr2, f_w3,
                      bias_d, w_out, b_out, ffn_args):
    bsz, seq, _ = x.shape
    n1, m1, m1_real, m3, m2f, m2i = _dft_tables(seq)
    n2 = DFT_N2
    width = n2 * D_MODEL
    pairs = bsz // 2
    sig4 = lambda a: a.reshape(pairs, 2, n1 // 2, width)
    five = lambda a: a.reshape(a.shape[0], 2, n1, n2, D_MODEL)

    filt = hy_filters(seq, f_w1, f_b1, f_fr1, f_w2, f_b2, f_fr2, f_w3)
    yield
    hspec = filter_spectrum(five(dft_stage1(filt.reshape(4, 1, n1 // 2, width), m1_real)), m2f)
    yield
    v, g1, g2 = hy_project(x, g_pre.reshape(1, -1), w_in.astype(BF16), b_in.reshape(1, -1), conv_w,
                           conv_b.reshape(1, -1))
    yield
    v4, g14, g24 = sig4(v), sig4(g1), sig4(g2)
    y = dft_stage1(v4, m1)
    yield
    z = conv_freq(five(y), m2f, m2i, hspec, 0).reshape(pairs, 2 * n1, width)
    yield
    skip = jnp.tile(bias_d, (1, HY_LANES // D_MODEL))
    z1, y = conv_time(z, v4, g14, skip[0:1], m3, m1)
    yield
    z = conv_freq(five(y), m2f, m2i, hspec, 1).reshape(pairs, 2 * n1, width)
    yield
    z2 = conv_time(z, z1, g24, skip[1:2], m3)
    yield
    t = bsz * seq
    args = (z2.reshape(t, D_MODEL), x.reshape(t, D_MODEL), w_out.astype(BF16), b_out.reshape(1, -1),
            g_post.reshape(1, -1))
    specs = [_row_spec(D_MODEL), _row_spec(D_MODEL), _const_spec((D_MODEL, D_MODEL)), _const_spec((1, D_MODEL)),
             _const_spec((1, D_MODEL))]
    yield tail_ffn_block(_hy_tail, args, specs, ffn_args, "hy_tail_ffn").reshape(bsz, seq, D_MODEL)


def _in_lockstep(generators):
    results = [None] * len(generators)
    live = list(range(len(generators)))
    while live:
        for idx in list(live):
            out = next(generators[idx], StopIteration)
            if out is StopIteration:
                live.remove(idx)
            elif out is not None:
                results[idx] = out
    return results


def kernel(x_prompt, x_sample, norm_g, ffn_w_gate, ffn_w_up, ffn_w_down, a_w_in, a_conv_w, a_a_log, a_dt_bias,
           a_norm_w, a_w_out, b_w_in, b_decay_logit, b_gn_w, b_w_out, c_w_in, c_b_in, c_conv_w, c_conv_b,
           c_f_w1, c_f_b1, c_f_freq1, c_f_w2, c_f_b2, c_f_freq2, c_f_w3, c_bias_d, c_w_out, c_b_out):
    depth = norm_g.shape[0]
    wg = ffn_w_gate.astype(BF16)
    wu = ffn_w_up.astype(BF16)
    wd = ffn_w_down.astype(BF16)

    def ffn_args(i, which):
        return (norm_g[i, 4 * which:4 * which + 2], wg[i, which], wu[i, which], wd[i, which])

    def mixers_then_ffn(xs, i):
        kind, j = i % 3, i // 3
        g_pre, g_post = norm_g[i, 2], norm_g[i, 3]
        ffn = ffn_args(i, 1)
        if kind == 0:
            return [gdn_mixer_block(x, g_pre, g_post, a_w_in[j], a_conv_w[j], a_a_log[j], a_dt_bias[j],
                                    a_norm_w[j], a_w_out[j], ffn) for x in xs]
        if kind == 1:
            return [retnet_mixer_block(x, g_pre, g_post, b_w_in[j], b_decay_logit[j], b_gn_w[j], b_w_out[j], ffn)
                    for x in xs]
        return _in_lockstep([
            hyena_mixer_block(x, g_pre, g_post, c_w_in[j], c_b_in[j], c_conv_w[j], c_conv_b[j], c_f_w1[j], c_f_b1[j],
                              c_f_freq1[j], c_f_w2[j], c_f_b2[j], c_f_freq2[j], c_f_w3[j], c_bias_d[j], c_w_out[j],
                              c_b_out[j], ffn) for x in xs])

    xs = [x_prompt, x_sample]
    for i in range(depth):
        xs = [ffn_block(x.reshape(-1, D_MODEL), *ffn_args(i, 0)).reshape(x.shape) for x in xs]
        xs = mixers_then_ffn(xs, i)
    return tuple(xs)
```

```python
import functools
import math

import numpy as np
import jax
import jax.numpy as jnp
from jax import lax
from jax.experimental import pallas as pl
from jax.experimental.pallas import tpu as pltpu

F32 = jnp.float32
BF16 = jnp.bfloat16
EPS = 1e-6

D_MODEL = 1024
D_FF = 2816

A_HEADS, A_DK, A_DV, A_CHUNK = 8, 128, 128, 64
A_QK = A_HEADS * A_DK
A_V = A_HEADS * A_DV
B_HEADS, B_DK, B_DV = 4, 256, 512
ROPE_BASE = 10000.0
C_EMB, C_FILTER_WIDTH = 33, 64
C_TARGET, C_SHORT_DECAY_PCT, C_LONG_DECAY_PCT = 1e-2, 0.3, 1.5
C_MIN_DECAY = math.log(C_TARGET) / C_LONG_DECAY_PCT
C_MAX_DECAY = math.log(C_TARGET) / C_SHORT_DECAY_PCT

V7X_VMEM_BYTES = 64 * 1024 * 1024
VMEM_LIMIT = V7X_VMEM_BYTES - 8 * 1024 * 1024
SUBLANE = 8
LANE = 128
MXU_DIM = 256

HALO = SUBLANE


def _params(*sem):
    return pltpu.CompilerParams(dimension_semantics=sem, vmem_limit_bytes=VMEM_LIMIT)


def _const_spec(shape):
    nd = len(shape)
    return pl.BlockSpec(shape, lambda *_: (0,) * nd, pipeline_mode=pl.Buffered(1))


def _rms(x, g):
    return x * lax.rsqrt(jnp.mean(x * x, axis=-1, keepdims=True) + EPS) * g


def _silu(x):
    return x * jax.nn.sigmoid(x)


def _dot(a, b):
    return jnp.dot(a, b, preferred_element_type=F32)


def _dot_nt(a, b):
    return lax.dot_general(a, b, (((1,), (1,)), ((), ())), preferred_element_type=F32)


def _dot_tn(a, b):
    return lax.dot_general(a, b, (((0,), (0,)), ((), ())), preferred_element_type=F32)


FFN_TM = 512
FFN_CHUNKS = 2


def _ffn_value(x, g, wg_ref, wu_ref, wd_ref):
    xn = _rms(x, g[0:1]).astype(BF16)
    fc = D_FF // FFN_CHUNKS
    y = None
    for c in range(FFN_CHUNKS):
        hg = _dot(xn, wg_ref[:, c * fc:(c + 1) * fc])
        hu = _dot(xn, wu_ref[:, c * fc:(c + 1) * fc])
        act = (_silu(hg) * hu).astype(BF16)
        part = _dot(act, wd_ref[c * fc:(c + 1) * fc, :])
        y = part if y is None else y + part
    return x + 0.5 * _rms(y, g[1:2])


def _in_lockstep(generators):
    results = [None] * len(generators)
    live = list(range(len(generators)))
    while live:
        for idx in list(live):
            out = next(generators[idx], StopIteration)
            if out is StopIteration:
                live.remove(idx)
            elif out is not None:
                results[idx] = out
    return results


def _ffn_kernel(x_ref, g_ref, wg_ref, wu_ref, wd_ref, o_ref):
    o_ref[...] = _ffn_value(x_ref[...], g_ref[...], wg_ref, wu_ref, wd_ref)


def _tail_ffn_kernel(*refs, tail, n_tail, n_tiles):
    g_ref, wg_ref, wu_ref, wd_ref, o_ref, x_scr = refs[n_tail:]
    step = pl.program_id(0)

    @pl.when(step == 0)
    def _():
        x_scr[...] = tail(*refs[:n_tail])

    @pl.when(jnp.logical_and(step > 0, step < n_tiles))
    def _():
        o_ref[...] = _ffn_value(x_scr[...], g_ref[...], wg_ref, wu_ref, wd_ref)
        x_scr[...] = tail(*refs[:n_tail])

    @pl.when(step == n_tiles)
    def _():
        o_ref[...] = _ffn_value(x_scr[...], g_ref[...], wg_ref, wu_ref, wd_ref)


def _ffn_specs():
    return [_const_spec((2, D_MODEL)), _const_spec((D_MODEL, D_FF)), _const_spec((D_MODEL, D_FF)),
            _const_spec((D_FF, D_MODEL))]


def tail_ffn_block(tail, tail_args, tail_specs, ffn_args, name):
    t = tail_args[0].shape[0]
    tm = FFN_TM
    n_tiles = t // tm
    return pl.pallas_call(
        functools.partial(_tail_ffn_kernel, tail=tail, n_tail=len(tail_args), n_tiles=n_tiles),
        grid=(n_tiles + 1,),
        in_specs=list(tail_specs) + _ffn_specs(),
        out_specs=pl.BlockSpec((tm, D_MODEL), lambda i: (jnp.maximum(i - 1, 0), 0)),
        out_shape=jax.ShapeDtypeStruct((t, D_MODEL), F32),
        scratch_shapes=[pltpu.VMEM((tm, D_MODEL), F32)],
        compiler_params=_params("arbitrary"),
        name=name,
    )(*tail_args, *ffn_args)


def _row_spec(cols):
    return pl.BlockSpec((FFN_TM, cols), lambda i: (i, 0))


def _tail_row_spec(cols, n_rows):
    last = n_rows // FFN_TM - 1
    return pl.BlockSpec((FFN_TM, cols), lambda i: (jnp.minimum(i, last), 0))


def ffn_block(x2, g_pair, wg, wu, wd):
    t = x2.shape[0]
    tm = FFN_TM
    return pl.pallas_call(
        _ffn_kernel,
        grid=(t // tm,),
        in_specs=[_row_spec(D_MODEL)] + _ffn_specs(),
        out_specs=_row_spec(D_MODEL),
        out_shape=jax.ShapeDtypeStruct((t, D_MODEL), F32),
        compiler_params=_params("parallel"),
        name="ffn_block",
    )(x2, g_pair, wg, wu, wd)


PROJ_TM = 512


def _halo_specs(tm, seq):
    per = tm // HALO
    last = seq // HALO - 1
    return [
        pl.BlockSpec((1, tm, D_MODEL), lambda b, i: (b, i, 0)),
        pl.BlockSpec((1, HALO, D_MODEL), lambda b, i: (b, jnp.maximum(i * per - 1, 0), 0)),
        pl.BlockSpec((1, HALO, D_MODEL), lambda b, i: (b, jnp.minimum((i + 1) * per, last), 0)),
    ]


def _normed_with_halo(xm_ref, xp_ref, xn_ref, g):
    xe = jnp.concatenate([xp_ref[0], xm_ref[0], xn_ref[0]], axis=0)
    return _rms(xe, g).astype(BF16)


def _zero_outside(p, tm, n_tiles):
    i = pl.program_id(1)
    before = jnp.where(i > 0, p[:HALO], 0.0)
    after = jnp.where(i < n_tiles - 1, p[HALO + tm:], 0.0)
    return jnp.concatenate([before, p[HALO:HALO + tm], after], axis=0)


def _conv3(p, cw, tm):
    rows = p.shape[0]
    prev = pltpu.roll(p, 1, 0)[HALO:HALO + tm]
    nxt = pltpu.roll(p, rows - 1, 0)[HALO:HALO + tm]
    return prev * cw[0:1] + p[HALO:HALO + tm] * cw[1:2] + nxt * cw[2:3]


GATE_LANES = LANE


def _chunk_scan(val, pos, size, axis, reverse):
    s = 1
    while s < A_CHUNK:
        if reverse:
            shifted = pltpu.roll(val, size - s, axis)
            val = val + jnp.where(pos < A_CHUNK - s, shifted, 0.0)
        else:
            shifted = pltpu.roll(val, s, axis)
            val = val + jnp.where(pos >= s, shifted, 0.0)
        s *= 2
    return val


def _gdn_proj_kernel(xm_ref, xp_ref, xn_ref, g_ref, wqkv_ref, wz_ref, wab_ref, cw_ref, alog_ref, dtb_ref,
                     q_ref, k_ref, v_ref, z_ref, gcol_ref, grow_ref, *, tm, n_tiles):
    xne = _normed_with_halo(xm_ref, xp_ref, xn_ref, g_ref[...])
    outs = (q_ref, k_ref, v_ref)
    for part in range(3):
        cols = slice(part * A_QK, (part + 1) * A_QK)
        p = _zero_outside(_dot(xne, wqkv_ref[:, cols]), tm, n_tiles)
        a = _silu(_conv3(p, cw_ref[:, cols], tm))
        if part == 2:
            v_ref[0] = a.astype(BF16)
            continue
        scale = A_DK ** -0.5 if part == 0 else 1.0
        for h in range(A_HEADS):
            ah = a[:, h * A_DK:(h + 1) * A_DK]
            inv = lax.rsqrt(jnp.sum(ah * ah, axis=-1, keepdims=True) + EPS) * scale
            outs[part][0, :, h * A_DK:(h + 1) * A_DK] = (ah * inv).astype(BF16)
    xn_main = xne[HALO:HALO + tm]
    z_ref[0] = _dot(xn_main, wz_ref[...]).astype(BF16)
    ab = _dot(xn_main, wab_ref[...])
    col = lax.broadcasted_iota(jnp.int32, (1, GATE_LANES), 1)
    is_decay = jnp.logical_and(col % 16 < 8, col < 32)
    sp = ab + dtb_ref[...]
    softplus = jnp.maximum(sp, 0.0) + jnp.log(1.0 + jnp.exp(-jnp.abs(sp)))
    val = jnp.where(is_decay, -jnp.exp(alog_ref[...]) * softplus, jax.nn.sigmoid(ab))
    pos = lax.broadcasted_iota(jnp.int32, (tm, 1), 0) % A_CHUNK
    fwd = _chunk_scan(val, pos, tm, 0, False)
    bwd = _chunk_scan(val, pos, tm, 0, True)
    gate = jnp.where(col < 8, fwd, jnp.where(jnp.logical_and(col >= 16, col < 24), bwd, val))
    gcol_ref[0] = gate
    grow_ref[0] = gate.T[0:4 * A_HEADS, :]


def gdn_project(x, g_pre, wqkv, wz, wab, conv_w, alog_row, dtb_row):
    bsz, seq, _ = x.shape
    tm = PROJ_TM
    nt = seq // tm
    act = lambda c: jax.ShapeDtypeStruct((bsz, seq, c), BF16)
    act_spec = lambda c: pl.BlockSpec((1, tm, c), lambda b, i: (b, i, 0))
    return pl.pallas_call(
        functools.partial(_gdn_proj_kernel, tm=tm, n_tiles=nt),
        grid=(bsz, nt),
        in_specs=_halo_specs(tm, seq) + [
            _const_spec((1, D_MODEL)),
            _const_spec((D_MODEL, 3 * A_QK)),
            _const_spec((D_MODEL, A_V)),
            _const_spec((D_MODEL, GATE_LANES)),
            _const_spec((3, 3 * A_QK)),
            _const_spec((1, GATE_LANES)),
            _const_spec((1, GATE_LANES)),
        ],
        out_specs=[act_spec(A_QK), act_spec(A_QK), act_spec(A_V), act_spec(A_V),
                   pl.BlockSpec((1, tm, GATE_LANES), lambda b, i: (b, i, 0)),
                   pl.BlockSpec((1, 4 * A_HEADS, tm), lambda b, i: (b, 0, i))],
        out_shape=[act(A_QK), act(A_QK), act(A_V), act(A_V),
                   jax.ShapeDtypeStruct((bsz, seq, GATE_LANES), F32),
                   jax.ShapeDtypeStruct((bsz, 4 * A_HEADS, seq), F32)],
        compiler_params=_params("parallel", "parallel"),
        name="gdn_project",
    )(x, x, x, g_pre, wqkv, wz, wab, conv_w, alog_row, dtb_row)


GDN_HPB = 8
GDN_BLK = 256


def _gdn_direction(q_ref, k_ref, v_ref, gcol_ref, grow_ref, s_ref, o_ref, hh, head, reverse):
    n = GDN_BLK
    hcols = slice(hh * A_DK, (hh + 1) * A_DK)
    q = q_ref[0, :, hcols].astype(F32)
    k = k_ref[0, :, hcols].astype(F32)
    v = v_ref[0, :, hcols].astype(F32)
    d = 1 if reverse else 0
    gcol = gcol_ref[0]
    lane = lax.broadcasted_iota(jnp.int32, (1, GATE_LANES), 1)
    pick = lambda c: jnp.sum(jnp.where(lane == c, gcol, 0.0), axis=-1, keepdims=True)
    gc = pick(d * 16 + head)
    beta = pick(d * 16 + 8 + head)
    gr = grow_ref[0, pl.ds(d * 16 + head, 1), :]

    ri = lax.broadcasted_iota(jnp.int32, (n, n), 0)
    ci = lax.broadcasted_iota(jnp.int32, (n, n), 1)
    same = (ri // A_CHUNK) == (ci // A_CHUNK)
    incl = jnp.logical_and(same, (ri <= ci) if reverse else (ri >= ci))
    strict = jnp.logical_and(incl, ri != ci)
    decay = jnp.where(incl, jnp.exp(jnp.where(incl, gc - gr, 0.0)), 0.0)

    kb = k * beta
    kb16 = kb.astype(BF16)
    k16 = k.astype(BF16)
    kk = _dot_nt(kb16, k16)
    qk = _dot_nt(q.astype(BF16), k16)
    yield
    a = jnp.where(strict, kk * decay, 0.0)
    attn16 = (qk * decay).astype(BF16)

    n_chunks = n // A_CHUNK
    side_by_side = lambda m: sum(m[c * A_CHUNK:(c + 1) * A_CHUNK] for c in range(n_chunks))
    block_diag16 = lambda m: jnp.where(same, jnp.concatenate([m] * n_chunks, axis=0), 0.0).astype(BF16)
    a_side = side_by_side(a)
    eye_side = (lax.broadcasted_iota(jnp.int32, (A_CHUNK, n), 0)
                == lax.broadcasted_iota(jnp.int32, (A_CHUNK, n), 1) % A_CHUNK)
    t_side = jnp.where(eye_side, 1.0, 0.0) - a_side
    a_pow = _dot(a_side.astype(BF16), a.astype(BF16))
    yield
    s = 2
    while s < A_CHUNK:
        ap16 = block_diag16(a_pow)
        upd = _dot(t_side.astype(BF16), ap16)
        s *= 2
        if s < A_CHUNK:
            a_pow = _dot(a_pow.astype(BF16), ap16)
        yield
        t_side = t_side + upd

    eg = jnp.exp(gc)
    uw = _dot(block_diag16(t_side), jnp.concatenate([v * beta, kb * eg], axis=1).astype(BF16))
    yield
    u = uw[:, :A_DV]
    w = uw[:, A_DV:]
    qd = q * eg

    state = s_ref[hh]
    n_chunks = n // A_CHUNK
    order = range(n_chunks - 1, -1, -1) if reverse else range(n_chunks)
    for c in order:
        r0 = c * A_CHUNK
        rows = slice(r0, r0 + A_CHUNK)
        last = r0 if reverse else r0 + A_CHUNK - 1
        g_last = gc[last:last + 1, :]
        s16 = state.astype(BF16)
        wq = _dot(jnp.concatenate([w[rows], qd[rows]], axis=0).astype(BF16), s16)
        yield
        v_new = u[rows] - wq[:A_CHUNK]
        vn16 = v_new.astype(BF16)
        intra = _dot(attn16[rows, r0:r0 + A_CHUNK], vn16)
        kd = k[rows] * jnp.exp(g_last - gc[rows])
        s_add = _dot_tn(kd.astype(BF16), vn16)
        yield
        o_ref[0, rows, hcols] = (wq[A_CHUNK:] + intra).astype(BF16)
        state = state * jnp.exp(g_last) + s_add
    s_ref[hh] = state


def _gdn_core_kernel(qf_ref, kf_ref, vf_ref, gcf_ref, grf_ref, qb_ref, kb_ref, vb_ref, gcb_ref, grb_ref,
                     of_ref, ob_ref, sf_ref, sb_ref):
    @pl.when(pl.program_id(2) == 0)
    def _():
        sf_ref[...] = jnp.zeros_like(sf_ref)
        sb_ref[...] = jnp.zeros_like(sb_ref)

    chains = []
    for hh in range(GDN_HPB):
        head = hh if GDN_HPB == A_HEADS else pl.program_id(1) * GDN_HPB + hh
        chains.append(_gdn_direction(qf_ref, kf_ref, vf_ref, gcf_ref, grf_ref, sf_ref, of_ref, hh, head, False))
        chains.append(_gdn_direction(qb_ref, kb_ref, vb_ref, gcb_ref, grb_ref, sb_ref, ob_ref, hh, head, True))
    while chains:
        chains = [c for c in chains if next(c, True) is None]


def gdn_core(q, k, v, gcol, grow):
    bsz, seq, _ = q.shape
    n = GDN_BLK
    nb = seq // n
    w = GDN_HPB * A_DK
    fwd = lambda b, h, j: (b, j, h)
    bwd = lambda b, h, j: (b, nb - 1 - j, h)
    specs = []
    for im in (fwd, bwd):
        specs += [pl.BlockSpec((1, n, w), im), pl.BlockSpec((1, n, w), im), pl.BlockSpec((1, n, w), im),
                  pl.BlockSpec((1, n, GATE_LANES), (lambda im: lambda b, h, j: (b, im(b, h, j)[1], 0))(im)),
                  pl.BlockSpec((1, 4 * A_HEADS, n), (lambda im: lambda b, h, j: (b, 0, im(b, h, j)[1]))(im))]
    out = jax.ShapeDtypeStruct((bsz, seq, A_V), BF16)
    return pl.pallas_call(
        _gdn_core_kernel,
        grid=(bsz, A_HEADS // GDN_HPB, nb),
        in_specs=specs,
        out_specs=[pl.BlockSpec((1, n, w), fwd), pl.BlockSpec((1, n, w), bwd)],
        out_shape=[out, out],
        scratch_shapes=[pltpu.VMEM((GDN_HPB, A_DK, A_DV), F32), pltpu.VMEM((GDN_HPB, A_DK, A_DV), F32)],
        compiler_params=_params("parallel", "parallel", "arbitrary"),
        name="gdn_core",
    )(q, k, v, gcol, grow, q, k, v, gcol, grow)


def _gdn_tail(of_ref, ob_ref, z_ref, x_ref, nw_ref, wo_ref, g_ref):
    nw = nw_ref[...]
    parts = []
    for h in range(A_HEADS):
        cols = slice(h * A_DV, (h + 1) * A_DV)
        o = of_ref[:, cols].astype(F32) + ob_ref[:, cols].astype(F32)
        o = o * lax.rsqrt(jnp.mean(o * o, axis=-1, keepdims=True) + EPS) * nw
        parts.append((o * _silu(z_ref[:, cols].astype(F32))).astype(BF16))
    hout = _dot(jnp.concatenate(parts, axis=1), wo_ref[...])
    return x_ref[...] + _rms(hout, g_ref[...])


def gdn_mixer_block(x, g_pre, g_post, w_in, conv_w, a_log, dt_bias, norm_w, w_out, ffn_args):
    bsz, seq, _ = x.shape
    n_qkv = 2 * A_QK + A_V
    wqkv = w_in[:, :n_qkv].astype(BF16)
    wz = w_in[:, n_qkv:n_qkv + A_V].astype(BF16)
    pad = GATE_LANES - 4 * A_HEADS
    wab = jnp.pad(w_in[:, n_qkv + A_V:], ((0, 0), (0, pad))).astype(BF16)
    zeros = jnp.zeros((2, A_HEADS), F32)
    alog_row = jnp.pad(jnp.stack([a_log, zeros], axis=1).reshape(1, 4 * A_HEADS), ((0, 0), (0, pad)))
    dtb_row = jnp.pad(jnp.stack([dt_bias, zeros], axis=1).reshape(1, 4 * A_HEADS), ((0, 0), (0, pad)))
    q, k, v, z, gcol, grow = gdn_project(x, g_pre.reshape(1, -1), wqkv, wz, wab, conv_w, alog_row, dtb_row)
    o_f, o_b = gdn_core(q, k, v, gcol, grow)
    t = bsz * seq
    args = (o_f.reshape(t, A_V), o_b.reshape(t, A_V), z.reshape(t, A_V), x.reshape(t, D_MODEL),
            norm_w.reshape(1, A_DV), w_out.astype(BF16), g_post.reshape(1, -1))
    row = functools.partial(_tail_row_spec, n_rows=t)
    specs = [row(A_V), row(A_V), row(A_V), row(D_MODEL), _const_spec((1, A_DV)),
             _const_spec((A_V, D_MODEL)), _const_spec((1, D_MODEL))]
    return tail_ffn_block(_gdn_tail, args, specs, ffn_args, "gdn_tail_ffn").reshape(bsz, seq, D_MODEL)


RET_BLK = 256
DECAY_ROWS = SUBLANE


def _log_gamma(dl_ref):
    return jnp.log1p(-jnp.exp2(dl_ref[...]))


def _ret_proj_kernel(x_ref, g_ref, w_ref, cos_ref, sin_ref, dl_ref,
                     q_ref, qf_ref, qb_ref, k_ref, kf_ref, kb_ref, v_ref, gate_ref, *, tm):
    xn = _rms(x_ref[0], g_ref[...]).astype(BF16)
    cos = cos_ref[...]
    sin = sin_ref[...]
    nq = B_HEADS * B_DK
    nv = B_HEADS * B_DV
    half = B_DK // 2
    n = RET_BLK
    log_gamma = _log_gamma(dl_ref)
    pos = (lax.broadcasted_iota(jnp.int32, (tm, half), 0) % n).astype(F32)
    for part in range(2):
        p = _dot(xn, w_ref[:, part * nq:(part + 1) * nq])
        for h in range(B_HEADS):
            lgf = log_gamma[0:1, h:h + 1]
            lgb = log_gamma[1:2, h:h + 1]
            x1 = p[:, h * B_DK:h * B_DK + half]
            x2 = p[:, h * B_DK + half:(h + 1) * B_DK]
            if part == 0:
                variants = ((q_ref, None), (qf_ref, jnp.exp(lgf * (pos + 1.0))), (qb_ref, jnp.exp(lgb * (n - pos))))
                scale = 1.0
            else:
                variants = ((k_ref, None), (kf_ref, jnp.exp(lgf * (n - 1.0 - pos))), (kb_ref, jnp.exp(lgb * pos)))
                scale = B_DK ** -0.5
            r1 = (x1 * cos - x2 * sin) * scale
            r2 = (x2 * cos + x1 * sin) * scale
            for ref, fac in variants:
                ref[0, :, h * B_DK:h * B_DK + half] = (r1 if fac is None else r1 * fac).astype(BF16)
                ref[0, :, h * B_DK + half:(h + 1) * B_DK] = (r2 if fac is None else r2 * fac).astype(BF16)
    v_ref[0] = _dot(xn, w_ref[:, 2 * nq:2 * nq + nv]).astype(BF16)
    gate_ref[0] = _dot(xn, w_ref[:, 2 * nq + nv:]).astype(BF16)


def ret_project(x, g_pre, w_in, cos, sin, decay_logit_tile):
    bsz, seq, _ = x.shape
    tm = PROJ_TM
    nq = B_HEADS * B_DK
    nv = B_HEADS * B_DV
    act = lambda c: jax.ShapeDtypeStruct((bsz, seq, c), BF16)
    act_spec = lambda c: pl.BlockSpec((1, tm, c), lambda b, i: (b, i, 0))
    tab_spec = pl.BlockSpec((tm, B_DK // 2), lambda b, i: (i, 0))
    return pl.pallas_call(
        functools.partial(_ret_proj_kernel, tm=tm),
        grid=(bsz, seq // tm),
        in_specs=[act_spec(D_MODEL), _const_spec((1, D_MODEL)), _const_spec((D_MODEL, 2 * nq + 2 * nv)),
                  tab_spec, tab_spec, _const_spec((DECAY_ROWS, LANE))],
        out_specs=[act_spec(nq)] * 6 + [act_spec(nv), act_spec(nv)],
        out_shape=[act(nq)] * 6 + [act(nv), act(nv)],
        compiler_params=_params("parallel", "parallel"),
        name="ret_project",
    )(x, g_pre, w_in, cos, sin, decay_logit_tile)


def _ret_forward(q_ref, k_ref, qx_ref, kz_ref, v_ref, o_ref, r_ref, mask_ref, h, g_blk):
    qcols = slice(h * B_DK, (h + 1) * B_DK)
    vcols = slice(h * B_DV, (h + 1) * B_DV)
    v = v_ref[0, :, vcols]
    scores = _dot_nt(q_ref[0, :, qcols], k_ref[0, :, qcols])
    r = r_ref[h]
    cross = _dot(qx_ref[0, :, qcols], r.astype(BF16))
    r_add = _dot_tn(kz_ref[0, :, qcols], v)
    yield
    inner = _dot((scores * mask_ref[h]).astype(BF16), v)
    r_ref[h] = r * g_blk + r_add
    yield
    o_ref[0, :, vcols] = (inner + cross).astype(BF16)


def _ret_backward(qx_ref, kz_ref, v_ref, o_ref, r_ref, h, g_blk):
    qcols = slice(h * B_DK, (h + 1) * B_DK)
    vcols = slice(h * B_DV, (h + 1) * B_DV)
    r = r_ref[h]
    cross = _dot(qx_ref[0, :, qcols], r.astype(BF16))
    r_add = _dot_tn(kz_ref[0, :, qcols], v_ref[0, :, vcols])
    yield
    o_ref[0, :, vcols] = cross.astype(BF16)
    r_ref[h] = r * g_blk + r_add


def _ret_core_kernel(dl_ref, q_ref, k_ref, qxf_ref, kzf_ref, vf_ref, qxb_ref, kzb_ref, vb_ref,
                     of_ref, ob_ref, rf_ref, rb_ref, mask_ref):
    n = RET_BLK
    log_gamma = _log_gamma(dl_ref)

    @pl.when(pl.program_id(1) == 0)
    def _():
        rf_ref[...] = jnp.zeros_like(rf_ref)
        rb_ref[...] = jnp.zeros_like(rb_ref)
        ri = lax.broadcasted_iota(jnp.int32, (n, n), 0)
        ci = lax.broadcasted_iota(jnp.int32, (n, n), 1)
        dist = (ri - ci).astype(F32)
        for h in range(B_HEADS):
            lgf = log_gamma[0:1, h:h + 1]
            lgb = log_gamma[1:2, h:h + 1]
            mask_ref[h] = (jnp.where(dist >= 0, jnp.exp(lgf * jnp.maximum(dist, 0.0)), 0.0)
                           + jnp.where(dist <= 0, jnp.exp(lgb * jnp.maximum(-dist, 0.0)), 0.0))

    g_blk = jnp.exp(log_gamma * float(n))
    chains = []
    for h in range(B_HEADS):
        chains.append(_ret_forward(q_ref, k_ref, qxf_ref, kzf_ref, vf_ref, of_ref, rf_ref, mask_ref, h,
                                   g_blk[0:1, h:h + 1]))
        chains.append(_ret_backward(qxb_ref, kzb_ref, vb_ref, ob_ref, rb_ref, h, g_blk[1:2, h:h + 1]))
    while chains:
        chains = [c for c in chains if next(c, True) is None]


def ret_core(decay_logit_tile, q, qxf, qxb, k, kzf, kzb, v):
    bsz, seq, _ = q.shape
    n = RET_BLK
    nb = seq // n
    nq = B_HEADS * B_DK
    nv = B_HEADS * B_DV
    fwd = lambda b, j: (b, j, 0)
    bwd = lambda b, j: (b, nb - 1 - j, 0)
    qk = lambda im: pl.BlockSpec((1, n, nq), im)
    val = lambda im: pl.BlockSpec((1, n, nv), im)
    out = jax.ShapeDtypeStruct((bsz, seq, nv), BF16)
    state = pltpu.VMEM((B_HEADS, B_DK, B_DV), F32)
    return pl.pallas_call(
        _ret_core_kernel,
        grid=(bsz, nb),
        in_specs=[_const_spec((DECAY_ROWS, LANE)), qk(fwd), qk(fwd), qk(fwd), qk(fwd), val(fwd),
                  qk(bwd), qk(bwd), val(bwd)],
        out_specs=[val(fwd), val(bwd)],
        out_shape=[out, out],
        scratch_shapes=[state, state, pltpu.VMEM((B_HEADS, n, n), F32)],
        compiler_params=_params("parallel", "arbitrary"),
        name="ret_core",
    )(decay_logit_tile, q, k, qxf, kzf, v, qxb, kzb, v)


def _ret_tail(of_ref, ob_ref, gate_ref, x_ref, gn_ref, wo_ref, g_ref):
    parts = []
    for h in range(B_HEADS):
        cols = slice(h * B_DV, (h + 1) * B_DV)
        o = of_ref[:, cols].astype(F32) + ob_ref[:, cols].astype(F32)
        oc = o - jnp.mean(o, axis=-1, keepdims=True)
        on = oc * lax.rsqrt(jnp.mean(oc * oc, axis=-1, keepdims=True) + EPS) * gn_ref[:, cols]
        parts.append((_silu(gate_ref[:, cols].astype(F32)) * on).astype(BF16))
    hout = _dot(jnp.concatenate(parts, axis=1), wo_ref[...])
    return x_ref[...] + _rms(hout, g_ref[...])


def _rope_tables(seq):
    inv = ROPE_BASE ** (-np.arange(0, B_DK, 2, dtype=np.float64) / B_DK)
    ang = np.arange(seq, dtype=np.float64)[:, None] * inv[None, :]
    return jnp.asarray(np.cos(ang), F32), jnp.asarray(np.sin(ang), F32)


def retnet_mixer_block(x, g_pre, g_post, w_in, decay_logit, gn_w, w_out, ffn_args):
    bsz, seq, _ = x.shape
    cos, sin = _rope_tables(seq)
    dl = jnp.pad(decay_logit, ((0, DECAY_ROWS - 2), (0, LANE - B_HEADS)), constant_values=-1.0)
    q, qxf, qxb, k, kzf, kzb, v, gate = ret_project(x, g_pre.reshape(1, -1), w_in.astype(BF16), cos, sin, dl)
    o_f, o_b = ret_core(dl, q, qxf, qxb, k, kzf, kzb, v)
    t = bsz * seq
    nv = B_HEADS * B_DV
    args = (o_f.reshape(t, nv), o_b.reshape(t, nv), gate.reshape(t, nv), x.reshape(t, D_MODEL),
            gn_w.reshape(1, nv), w_out.astype(BF16), g_post.reshape(1, -1))
    row = functools.partial(_tail_row_spec, n_rows=t)
    specs = [row(nv), row(nv), row(nv), row(D_MODEL), _const_spec((1, nv)),
             _const_spec((nv, D_MODEL)), _const_spec((1, D_MODEL))]
    return tail_ffn_block(_ret_tail, args, specs, ffn_args, "ret_tail_ffn").reshape(bsz, seq, D_MODEL)


DFT_N2 = 128
HY_LANES = 4 * D_MODEL
HY_CB = D_MODEL
HY_PAIRS = 2
FILTER_TL = 256


def _dft_tables(seq):
    n = 2 * seq
    n2 = DFT_N2
    n1 = n // n2
    kk = np.arange(n1, dtype=np.float64)[:, None]
    nn = np.arange(n1 // 2, dtype=np.float64)[None, :]
    ang = 2.0 * np.pi * kk * nn / n1
    cc, ss = np.cos(ang), np.sin(ang)
    m1 = np.block([[cc, ss], [-ss, cc]])
    m1_real = np.concatenate([cc, -ss], axis=0)
    m3 = m1.T / n
    k1 = np.arange(n1, dtype=np.float64)[:, None, None]
    k2 = np.arange(n2, dtype=np.float64)[None, :, None]
    t2 = np.arange(n2, dtype=np.float64)[None, None, :]
    theta = 2.0 * np.pi * (t2 * k2 / n2 + t2 * k1 / n)
    c, s = np.cos(theta), np.sin(theta)
    m2f = np.concatenate([np.concatenate([c, s], axis=2), np.concatenate([-s, c], axis=2)], axis=1)
    m2i = np.transpose(m2f, (0, 2, 1))
    cast = lambda a: jnp.asarray(a, F32).astype(BF16)
    return n1, cast(m1), cast(m1_real), cast(m3), cast(m2f), cast(m2i)


def _hy_proj_kernel(xm_ref, xp_ref, xn_ref, g_ref, w_ref, bin_ref, cw_ref, cb_ref,
                    v_ref, g1_ref, g2_ref, *, tm, n_tiles):
    xne = _normed_with_halo(xm_ref, xp_ref, xn_ref, g_ref[...])
    for part, ref in enumerate((v_ref, g1_ref, g2_ref)):
        cols = slice(part * D_MODEL, (part + 1) * D_MODEL)
        p = _zero_outside(_dot(xne, w_ref[:, cols]) + bin_ref[:, cols], tm, n_tiles)
        ref[0] = (_conv3(p, cw_ref[:, cols], tm) + cb_ref[:, cols]).astype(BF16)


def hy_project(x, g_pre, w_in, b_in, conv_w, conv_b):
    bsz, seq, _ = x.shape
    tm = PROJ_TM
    nt = seq // tm
    out = jax.ShapeDtypeStruct((bsz, seq, D_MODEL), BF16)
    spec = pl.BlockSpec((1, tm, D_MODEL), lambda b, i: (b, i, 0))
    return pl.pallas_call(
        functools.partial(_hy_proj_kernel, tm=tm, n_tiles=nt),
        grid=(bsz, nt),
        in_specs=_halo_specs(tm, seq) + [
            _const_spec((1, D_MODEL)), _const_spec((D_MODEL, 3 * D_MODEL)), _const_spec((1, 3 * D_MODEL)),
            _const_spec((3, 3 * D_MODEL)), _const_spec((1, 3 * D_MODEL))],
        out_specs=[spec, spec, spec],
        out_shape=[out, out, out],
        compiler_params=_params("parallel", "parallel"),
        name="hy_project",
    )(x, x, x, g_pre, w_in, b_in, conv_w, conv_b)


def _dot_f32(a, b):
    return jnp.dot(a, b, preferred_element_type=F32, precision=lax.Precision.HIGHEST)


def _hy_filter_kernel(f_ref, w1t_ref, w1c_ref, w1s_ref, b1_ref, fr1_ref, w2_ref, b2_ref, fr2_ref, w3_ref,
                      delta_ref, h_ref, *, tl, seq):
    n = (lax.broadcasted_iota(jnp.int32, (tl, 1), 0) + pl.program_id(0) * tl).astype(F32)
    t = n / (seq - 1.0)
    arg = (2.0 * math.pi / seq) * n * f_ref[...]
    pre = (t * w1t_ref[...] + _dot_f32(jnp.cos(arg), w1c_ref[...]) + _dot_f32(-jnp.sin(arg), w1s_ref[...])
           + b1_ref[...])
    h1 = jnp.sin(fr1_ref[...] * pre)
    h2 = jnp.sin(fr2_ref[...] * (_dot_f32(h1, w2_ref[...]) + b2_ref[...]))
    h3 = _dot_f32(h2, w3_ref[...])
    window = jnp.exp(-t * delta_ref[...])
    for s in range(4):
        hs = h3[:, s * D_MODEL:(s + 1) * D_MODEL] * window
        if s >= 2:
            hs = jnp.where(n == 0.0, 0.0, hs)
        h_ref[s] = hs


def hy_filters(seq, f_w1, f_b1, f_fr1, f_w2, f_b2, f_fr2, f_w3):
    bands = (C_EMB - 1) // 2
    fw = C_FILTER_WIDTH
    f = np.zeros((1, LANE), np.float32)
    f[0, :bands] = np.linspace(1e-4, bands - 1, bands, dtype=np.float32)
    deltas = np.abs(np.linspace(C_MIN_DECAY, C_MAX_DECAY, D_MODEL, dtype=np.float32))[None, :]
    pad = ((0, LANE - bands), (0, 0))
    w1c = jnp.pad(f_w1[1:1 + bands], pad)
    w1s = jnp.pad(f_w1[1 + bands:], pad)
    tl = FILTER_TL
    row = lambda a: a.reshape(1, -1)
    args = (jnp.asarray(f), f_w1[0:1], w1c, w1s, row(f_b1), row(f_fr1), f_w2, row(f_b2), row(f_fr2), f_w3,
            jnp.asarray(deltas))
    return pl.pallas_call(
        functools.partial(_hy_filter_kernel, tl=tl, seq=seq),
        grid=(seq // tl,),
        in_specs=[_const_spec(a.shape) for a in args],
        out_specs=pl.BlockSpec((4, tl, D_MODEL), lambda i: (0, i, 0)),
        out_shape=jax.ShapeDtypeStruct((4, seq, D_MODEL), F32),
        compiler_params=_params("parallel"),
        name="hy_filters",
    )(*args)


def _dft1_kernel(u_ref, m1_ref, y_ref):
    p, half, lw = u_ref.shape[1:]
    y_ref[0] = _dot(m1_ref[...], u_ref[0].reshape(p * half, lw).astype(BF16)).astype(BF16)


def dft_stage1(u4, m1):
    g, p, half, width = u4.shape
    rows = m1.shape[0]
    lw = HY_LANES
    return pl.pallas_call(
        _dft1_kernel,
        grid=(g, width // lw),
        in_specs=[pl.BlockSpec((1, p, half, lw), lambda b, j: (b, 0, 0, j)), _const_spec(m1.shape)],
        out_specs=pl.BlockSpec((1, rows, lw), lambda b, j: (b, 0, j)),
        out_shape=jax.ShapeDtypeStruct((g, rows, width), BF16),
        compiler_params=_params("parallel", "parallel"),
        name="dft_stage1",
    )(u4, m1)


def _filter_spec_kernel(y_ref, m2f_ref, h_ref):
    n2, cb = y_ref.shape[3:]
    x = [_dot(m2f_ref[0], y_ref[s, :, 0].reshape(2 * n2, cb)) for s in range(4)]
    for order in range(2):
        fwd, rev = x[order], x[2 + order]
        h_ref[order, 0, 0] = fwd[:n2] + rev[:n2]
        h_ref[order, 1, 0] = fwd[n2:] - rev[n2:]


def filter_spectrum(y5, m2f):
    s, _, n1, n2, c = y5.shape
    cb = HY_CB
    return pl.pallas_call(
        _filter_spec_kernel,
        grid=(n1, c // cb),
        in_specs=[pl.BlockSpec((s, 2, 1, n2, cb), lambda k, j: (0, 0, k, 0, j)),
                  pl.BlockSpec((1, 2 * n2, 2 * n2), lambda k, j: (k, 0, 0))],
        out_specs=pl.BlockSpec((2, 2, 1, n2, cb), lambda k, j: (0, 0, k, 0, j)),
        out_shape=jax.ShapeDtypeStruct((2, 2, n1, n2, c), F32),
        compiler_params=_params("parallel", "parallel"),
        name="filter_spectrum",
    )(y5, m2f)


def _conv_freq_kernel(y_ref, m2f_ref, m2i_ref, h_ref, z_ref):
    n2, cb = y_ref.shape[3:]
    hr = h_ref[0, 0, 0]
    hi = h_ref[0, 1, 0]
    for p in range(HY_PAIRS):
        x = _dot(m2f_ref[0], y_ref[p, :, 0].reshape(2 * n2, cb))
        xr, xi = x[:n2], x[n2:]
        prod = jnp.concatenate([xr * hr - xi * hi, xr * hi + xi * hr], axis=0).astype(BF16)
        z_ref[p, :, 0] = _dot(m2i_ref[0], prod).reshape(2, n2, cb).astype(BF16)


def conv_freq(y5, m2f, m2i, hspec, order):
    g, _, n1, n2, c = y5.shape
    cb = HY_CB
    blk = pl.BlockSpec((HY_PAIRS, 2, 1, n2, cb), lambda k, j, b: (b, 0, k, 0, j))
    tab = pl.BlockSpec((1, 2 * n2, 2 * n2), lambda k, j, b: (k, 0, 0))
    hblk = pl.BlockSpec((1, 2, 1, n2, cb), lambda k, j, b: (order, 0, k, 0, j))
    return pl.pallas_call(
        _conv_freq_kernel,
        grid=(n1, c // cb, g // HY_PAIRS),
        in_specs=[blk, tab, tab, hblk],
        out_specs=blk,
        out_shape=jax.ShapeDtypeStruct(y5.shape, BF16),
        compiler_params=_params("parallel", "parallel", "parallel"),
        name="conv_freq",
    )(y5, m2f, m2i, hspec)


def _conv_time_kernel(z_ref, u_ref, gate_ref, skip_ref, m3_ref, *rest, chain):
    half, lw = u_ref.shape[2:]
    y = _dot(m3_ref[...], z_ref[0])
    u = u_ref[0].reshape(2 * half, lw).astype(F32)
    out = (gate_ref[0].reshape(2 * half, lw).astype(F32) * (y + skip_ref[...] * u)).astype(BF16)
    if chain:
        m1_ref, o_ref, y_ref = rest
        y_ref[0] = _dot(m1_ref[...], out).astype(BF16)
    else:
        (o_ref,) = rest
    o_ref[0] = out.reshape(2, half, lw)


def conv_time(z3, u4, gate4, skip, m3, m1=None):
    g, _, half, width = u4.shape
    rows = z3.shape[1]
    lw = HY_LANES
    chain = m1 is not None
    sig = pl.BlockSpec((1, 2, half, lw), lambda b, j: (b, 0, 0, j))
    spec = pl.BlockSpec((1, rows, lw), lambda b, j: (b, 0, j))
    in_specs = [spec, sig, sig, _const_spec((1, lw)), _const_spec(m3.shape)]
    out_specs = [sig]
    out_shape = [jax.ShapeDtypeStruct(u4.shape, BF16)]
    args = [z3, u4, gate4, skip, m3]
    if chain:
        in_specs.append(_const_spec(m1.shape))
        out_specs.append(spec)
        out_shape.append(jax.ShapeDtypeStruct(z3.shape, BF16))
        args.append(m1)
    res = pl.pallas_call(
        functools.partial(_conv_time_kernel, chain=chain),
        grid=(g, width // lw),
        in_specs=in_specs,
        out_specs=out_specs,
        out_shape=out_shape,
        compiler_params=_params("parallel", "parallel"),
        name="conv_time_chain" if chain else "conv_time",
    )(*args)
    return res if chain else res[0]


def _hy_tail(z_ref, x_ref, wo_ref, bo_ref, g_ref):
    hout = _dot(z_ref[...], wo_ref[...]) + bo_ref[...]
    return x_ref[...] + _rms(hout, g_ref[...])


def hyena_mixer_block(x, g_pre, g_post, w_in, b_in, conv_w, conv_b, f_w1, f_b1, f_fr1, f_w2, f_b2, f_fr2, f_w3,
                      bias_d, w_out, b_out, ffn_args):
    bsz, seq, _ = x.shape
    n1, m1, m1_real, m3, m2f, m2i = _dft_tables(seq)
    n2 = DFT_N2
    width = n2 * D_MODEL
    pairs = bsz // 2
    sig4 = lambda a: a.reshape(pairs, 2, n1 // 2, width)
    five = lambda a: a.reshape(a.shape[0], 2, n1, n2, D_MODEL)

    filt = hy_filters(seq, f_w1, f_b1, f_fr1, f_w2, f_b2, f_fr2, f_w3)
    yield
    hspec = filter_spectrum(five(dft_stage1(filt.reshape(4, 1, n1 // 2, width), m1_real)), m2f)
    yield
    v, g1, g2 = hy_project(x, g_pre.reshape(1, -1), w_in.astype(BF16), b_in.reshape(1, -1), conv_w,
                           conv_b.reshape(1, -1))
    yield
    v4, g14, g24 = sig4(v), sig4(g1), sig4(g2)
    y = dft_stage1(v4, m1)
    yield
    z = conv_freq(five(y), m2f, m2i, hspec, 0).reshape(pairs, 2 * n1, width)
    yield
    skip = jnp.tile(bias_d, (1, HY_LANES // D_MODEL))
    z1, y = conv_time(z, v4, g14, skip[0:1], m3, m1)
    yield
    z = conv_freq(five(y), m2f, m2i, hspec, 1).reshape(pairs, 2 * n1, width)
    yield
    z2 = conv_time(z, z1, g24, skip[1:2], m3)
    yield
    t = bsz * seq
    args = (z2.reshape(t, D_MODEL), x.reshape(t, D_MODEL), w_out.astype(BF16), b_out.reshape(1, -1),
            g_post.reshape(1, -1))
    row = functools.partial(_tail_row_spec, n_rows=t)
    specs = [row(D_MODEL), row(D_MODEL), _const_spec((D_MODEL, D_MODEL)), _const_spec((1, D_MODEL)),
             _const_spec((1, D_MODEL))]
    yield tail_ffn_block(_hy_tail, args, specs, ffn_args, "hy_tail_ffn").reshape(bsz, seq, D_MODEL)


def kernel(x_prompt, x_sample, norm_g, ffn_w_gate, ffn_w_up, ffn_w_down, a_w_in, a_conv_w, a_a_log, a_dt_bias,
           a_norm_w, a_w_out, b_w_in, b_decay_logit, b_gn_w, b_w_out, c_w_in, c_b_in, c_conv_w, c_conv_b,
           c_f_w1, c_f_b1, c_f_freq1, c_f_w2, c_f_b2, c_f_freq2, c_f_w3, c_bias_d, c_w_out, c_b_out):
    depth = norm_g.shape[0]
    wg = ffn_w_gate.astype(BF16)
    wu = ffn_w_up.astype(BF16)
    wd = ffn_w_down.astype(BF16)

    def ffn_args(i, which):
        return (norm_g[i, 4 * which:4 * which + 2], wg[i, which], wu[i, which], wd[i, which])

    def mixers_then_ffn(xs, i):
        kind, j = i % 3, i // 3
        g_pre, g_post = norm_g[i, 2], norm_g[i, 3]
        ffn = ffn_args(i, 1)
        if kind == 0:
            return [gdn_mixer_block(x, g_pre, g_post, a_w_in[j], a_conv_w[j], a_a_log[j], a_dt_bias[j],
                                    a_norm_w[j], a_w_out[j], ffn) for x in xs]
        if kind == 1:
            return [retnet_mixer_block(x, g_pre, g_post, b_w_in[j], b_decay_logit[j], b_gn_w[j], b_w_out[j], ffn)
                    for x in xs]
        return _in_lockstep([
            hyena_mixer_block(x, g_pre, g_post, c_w_in[j], c_b_in[j], c_conv_w[j], c_conv_b[j], c_f_w1[j], c_f_b1[j],
                              c_f_freq1[j], c_f_w2[j], c_f_b2[j], c_f_freq2[j], c_f_w3[j], c_bias_d[j], c_w_out[j],
                              c_b_out[j], ffn) for x in xs])

    xs = [x_prompt, x_sample]
    for i in range(depth):
        xs = [ffn_block(x.reshape(-1, D_MODEL), *ffn_args(i, 0)).reshape(x.shape) for x in xs]
        xs = mixers_then_ffn(xs, i)
    return tuple(xs)
```

```python
import functools
import math

import numpy as np
import jax
import jax.numpy as jnp
from jax import lax
from jax.experimental import pallas as pl
from jax.experimental.pallas import tpu as pltpu

F32 = jnp.float32
BF16 = jnp.bfloat16
EPS = 1e-6

D_MODEL = 1024
D_FF = 2816

A_HEADS, A_DK, A_DV, A_CHUNK = 8, 128, 128, 64
A_QK = A_HEADS * A_DK
A_V = A_HEADS * A_DV
B_HEADS, B_DK, B_DV = 4, 256, 512
ROPE_BASE = 10000.0
C_EMB, C_FILTER_WIDTH = 33, 64
C_TARGET, C_SHORT_DECAY_PCT, C_LONG_DECAY_PCT = 1e-2, 0.3, 1.5
C_MIN_DECAY = math.log(C_TARGET) / C_LONG_DECAY_PCT
C_MAX_DECAY = math.log(C_TARGET) / C_SHORT_DECAY_PCT

V7X_VMEM_BYTES = 64 * 1024 * 1024
VMEM_LIMIT = V7X_VMEM_BYTES - 8 * 1024 * 1024
SUBLANE = 8
LANE = 128
MXU_DIM = 256

HALO = SUBLANE


def _params(*sem):
    return pltpu.CompilerParams(dimension_semantics=sem, vmem_limit_bytes=VMEM_LIMIT)


def _const_spec(shape):
    nd = len(shape)
    return pl.BlockSpec(shape, lambda *_: (0,) * nd, pipeline_mode=pl.Buffered(1))


def _rms(x, g):
    return x * lax.rsqrt(jnp.mean(x * x, axis=-1, keepdims=True) + EPS) * g


def _silu(x):
    return x * jax.nn.sigmoid(x)


def _dot(a, b):
    return jnp.dot(a, b, preferred_element_type=F32)


def _dot_nt(a, b):
    return lax.dot_general(a, b, (((1,), (1,)), ((), ())), preferred_element_type=F32)


def _dot_tn(a, b):
    return lax.dot_general(a, b, (((0,), (0,)), ((), ())), preferred_element_type=F32)


FFN_TM = 512
FFN_CHUNKS = 2


def _ffn_value(x, g, wg_ref, wu_ref, wd_ref):
    xn = _rms(x, g[0:1]).astype(BF16)
    fc = D_FF // FFN_CHUNKS
    y = None
    for c in range(FFN_CHUNKS):
        hg = _dot(xn, wg_ref[:, c * fc:(c + 1) * fc])
        hu = _dot(xn, wu_ref[:, c * fc:(c + 1) * fc])
        act = (_silu(hg) * hu).astype(BF16)
        part = _dot(act, wd_ref[c * fc:(c + 1) * fc, :])
        y = part if y is None else y + part
    return x + 0.5 * _rms(y, g[1:2])


def _in_lockstep(generators):
    results = [None] * len(generators)
    live = list(range(len(generators)))
    while live:
        for idx in list(live):
            out = next(generators[idx], StopIteration)
            if out is StopIteration:
                live.remove(idx)
            elif out is not None:
                results[idx] = out
    return results


def _ffn_kernel(x_ref, g_ref, wg_ref, wu_ref, wd_ref, o_ref):
    o_ref[...] = _ffn_value(x_ref[...], g_ref[...], wg_ref, wu_ref, wd_ref)


def _tail_ffn_kernel(*refs, tail, n_tail, n_tiles):
    g_ref, wg_ref, wu_ref, wd_ref, o_ref, x_scr = refs[n_tail:]
    step = pl.program_id(0)

    @pl.when(step == 0)
    def _():
        x_scr[...] = tail(*refs[:n_tail])

    @pl.when(jnp.logical_and(step > 0, step < n_tiles))
    def _():
        o_ref[...] = _ffn_value(x_scr[...], g_ref[...], wg_ref, wu_ref, wd_ref)
        x_scr[...] = tail(*refs[:n_tail])

    @pl.when(step == n_tiles)
    def _():
        o_ref[...] = _ffn_value(x_scr[...], g_ref[...], wg_ref, wu_ref, wd_ref)


def _ffn_specs():
    return [_const_spec((2, D_MODEL)), _const_spec((D_MODEL, D_FF)), _const_spec((D_MODEL, D_FF)),
            _const_spec((D_FF, D_MODEL))]


def tail_ffn_block(tail, tail_args, tail_specs, ffn_args, name):
    t = tail_args[0].shape[0]
    tm = FFN_TM
    n_tiles = t // tm
    return pl.pallas_call(
        functools.partial(_tail_ffn_kernel, tail=tail, n_tail=len(tail_args), n_tiles=n_tiles),
        grid=(n_tiles + 1,),
        in_specs=list(tail_specs) + _ffn_specs(),
        out_specs=pl.BlockSpec((tm, D_MODEL), lambda i: (jnp.maximum(i - 1, 0), 0)),
        out_shape=jax.ShapeDtypeStruct((t, D_MODEL), F32),
        scratch_shapes=[pltpu.VMEM((tm, D_MODEL), F32)],
        compiler_params=_params("arbitrary"),
        name=name,
    )(*tail_args, *ffn_args)


def _row_spec(cols):
    return pl.BlockSpec((FFN_TM, cols), lambda i: (i, 0))


def _tail_row_spec(cols, n_rows):
    last = n_rows // FFN_TM - 1
    return pl.BlockSpec((FFN_TM, cols), lambda i: (jnp.minimum(i, last), 0))


def ffn_block(x2, g_pair, wg, wu, wd):
    t = x2.shape[0]
    tm = FFN_TM
    return pl.pallas_call(
        _ffn_kernel,
        grid=(t // tm,),
        in_specs=[_row_spec(D_MODEL)] + _ffn_specs(),
        out_specs=_row_spec(D_MODEL),
        out_shape=jax.ShapeDtypeStruct((t, D_MODEL), F32),
        compiler_params=_params("parallel"),
        name="ffn_block",
    )(x2, g_pair, wg, wu, wd)


PROJ_TM = 512


def _halo_specs(tm, seq):
    per = tm // HALO
    last = seq // HALO - 1
    return [
        pl.BlockSpec((1, tm, D_MODEL), lambda b, i: (b, i, 0)),
        pl.BlockSpec((1, HALO, D_MODEL), lambda b, i: (b, jnp.maximum(i * per - 1, 0), 0)),
        pl.BlockSpec((1, HALO, D_MODEL), lambda b, i: (b, jnp.minimum((i + 1) * per, last), 0)),
    ]


def _normed_with_halo(xm_ref, xp_ref, xn_ref, g):
    xe = jnp.concatenate([xp_ref[0], xm_ref[0], xn_ref[0]], axis=0)
    return _rms(xe, g).astype(BF16)


def _zero_outside(p, tm, n_tiles):
    i = pl.program_id(1)
    before = jnp.where(i > 0, p[:HALO], 0.0)
    after = jnp.where(i < n_tiles - 1, p[HALO + tm:], 0.0)
    return jnp.concatenate([before, p[HALO:HALO + tm], after], axis=0)


def _conv3(p, cw, tm):
    rows = p.shape[0]
    prev = pltpu.roll(p, 1, 0)[HALO:HALO + tm]
    nxt = pltpu.roll(p, rows - 1, 0)[HALO:HALO + tm]
    return prev * cw[0:1] + p[HALO:HALO + tm] * cw[1:2] + nxt * cw[2:3]


GATE_LANES = LANE


def _chunk_scan(val, pos, size, axis, reverse):
    s = 1
    while s < A_CHUNK:
        if reverse:
            shifted = pltpu.roll(val, size - s, axis)
            val = val + jnp.where(pos < A_CHUNK - s, shifted, 0.0)
        else:
            shifted = pltpu.roll(val, s, axis)
            val = val + jnp.where(pos >= s, shifted, 0.0)
        s *= 2
    return val


def _gdn_proj_kernel(xm_ref, xp_ref, xn_ref, g_ref, wqkv_ref, wz_ref, wab_ref, cw_ref, alog_ref, dtb_ref,
                     q_ref, k_ref, v_ref, z_ref, gcol_ref, grow_ref, *, tm, n_tiles):
    xne = _normed_with_halo(xm_ref, xp_ref, xn_ref, g_ref[...])
    outs = (q_ref, k_ref, v_ref)
    for part in range(3):
        cols = slice(part * A_QK, (part + 1) * A_QK)
        p = _zero_outside(_dot(xne, wqkv_ref[:, cols]), tm, n_tiles)
        a = _silu(_conv3(p, cw_ref[:, cols], tm))
        if part == 2:
            v_ref[0] = a.astype(BF16)
            continue
        scale = A_DK ** -0.5 if part == 0 else 1.0
        for h in range(A_HEADS):
            ah = a[:, h * A_DK:(h + 1) * A_DK]
            inv = lax.rsqrt(jnp.sum(ah * ah, axis=-1, keepdims=True) + EPS) * scale
            outs[part][0, :, h * A_DK:(h + 1) * A_DK] = (ah * inv).astype(BF16)
    xn_main = xne[HALO:HALO + tm]
    z_ref[0] = _dot(xn_main, wz_ref[...]).astype(BF16)
    ab = _dot(xn_main, wab_ref[...])
    col = lax.broadcasted_iota(jnp.int32, (1, GATE_LANES), 1)
    is_decay = jnp.logical_and(col % 16 < 8, col < 32)
    sp = ab + dtb_ref[...]
    softplus = jnp.maximum(sp, 0.0) + jnp.log(1.0 + jnp.exp(-jnp.abs(sp)))
    val = jnp.where(is_decay, -jnp.exp(alog_ref[...]) * softplus, jax.nn.sigmoid(ab))
    pos = lax.broadcasted_iota(jnp.int32, (tm, 1), 0) % A_CHUNK
    fwd = _chunk_scan(val, pos, tm, 0, False)
    bwd = _chunk_scan(val, pos, tm, 0, True)
    gate = jnp.where(col < 8, fwd, jnp.where(jnp.logical_and(col >= 16, col < 24), bwd, val))
    gcol_ref[0] = gate
    grow_ref[0] = gate.T[0:4 * A_HEADS, :]


def gdn_project(x, g_pre, wqkv, wz, wab, conv_w, alog_row, dtb_row):
    bsz, seq, _ = x.shape
    tm = PROJ_TM
    nt = seq // tm
    act = lambda c: jax.ShapeDtypeStruct((bsz, seq, c), BF16)
    act_spec = lambda c: pl.BlockSpec((1, tm, c), lambda b, i: (b, i, 0))
    return pl.pallas_call(
        functools.partial(_gdn_proj_kernel, tm=tm, n_tiles=nt),
        grid=(bsz, nt),
        in_specs=_halo_specs(tm, seq) + [
            _const_spec((1, D_MODEL)),
            _const_spec((D_MODEL, 3 * A_QK)),
            _const_spec((D_MODEL, A_V)),
            _const_spec((D_MODEL, GATE_LANES)),
            _const_spec((3, 3 * A_QK)),
            _const_spec((1, GATE_LANES)),
            _const_spec((1, GATE_LANES)),
        ],
        out_specs=[act_spec(A_QK), act_spec(A_QK), act_spec(A_V), act_spec(A_V),
                   pl.BlockSpec((1, tm, GATE_LANES), lambda b, i: (b, i, 0)),
                   pl.BlockSpec((1, 4 * A_HEADS, tm), lambda b, i: (b, 0, i))],
        out_shape=[act(A_QK), act(A_QK), act(A_V), act(A_V),
                   jax.ShapeDtypeStruct((bsz, seq, GATE_LANES), F32),
                   jax.ShapeDtypeStruct((bsz, 4 * A_HEADS, seq), F32)],
        compiler_params=_params("parallel", "parallel"),
        name="gdn_project",
    )(x, x, x, g_pre, wqkv, wz, wab, conv_w, alog_row, dtb_row)


GDN_HPB = 8
GDN_BASE = 8
GDN_BLK = 256


def _gdn_direction(q_ref, k_ref, v_ref, gcol_ref, grow_ref, s_ref, o_ref, hh, head, reverse):
    n = GDN_BLK
    hcols = slice(hh * A_DK, (hh + 1) * A_DK)
    q = q_ref[0, :, hcols].astype(F32)
    k = k_ref[0, :, hcols].astype(F32)
    v = v_ref[0, :, hcols].astype(F32)
    d = 1 if reverse else 0
    gcol = gcol_ref[0]
    lane = lax.broadcasted_iota(jnp.int32, (1, GATE_LANES), 1)
    pick = lambda c: jnp.sum(jnp.where(lane == c, gcol, 0.0), axis=-1, keepdims=True)
    gc = pick(d * 16 + head)
    beta = pick(d * 16 + 8 + head)
    gr = grow_ref[0, pl.ds(d * 16 + head, 1), :]

    ri = lax.broadcasted_iota(jnp.int32, (n, n), 0)
    ci = lax.broadcasted_iota(jnp.int32, (n, n), 1)
    same = (ri // A_CHUNK) == (ci // A_CHUNK)
    incl = jnp.logical_and(same, (ri <= ci) if reverse else (ri >= ci))
    strict = jnp.logical_and(incl, ri != ci)
    decay = jnp.where(incl, jnp.exp(jnp.where(incl, gc - gr, 0.0)), 0.0)

    kb = k * beta
    kb16 = kb.astype(BF16)
    k16 = k.astype(BF16)
    kk = _dot_nt(kb16, k16)
    qk = _dot_nt(q.astype(BF16), k16)
    yield
    a = jnp.where(strict, kk * decay, 0.0)
    attn16 = (qk * decay).astype(BF16)

    n_chunks = n // A_CHUNK
    side_by_side = lambda m: sum(m[c * A_CHUNK:(c + 1) * A_CHUNK] for c in range(n_chunks))
    block_diag16 = lambda m: jnp.where(same, jnp.concatenate([m] * n_chunks, axis=0), 0.0).astype(BF16)
    a_side = side_by_side(a)
    rs = lax.broadcasted_iota(jnp.int32, (A_CHUNK, n), 0)
    cs = lax.broadcasted_iota(jnp.int32, (A_CHUNK, n), 1) % A_CHUNK
    a_base = jnp.where(rs // GDN_BASE == cs // GDN_BASE, a_side, 0.0)
    t_side = jnp.where(rs == cs, 1.0, 0.0) - a_base
    a_pow = _dot(a_base.astype(BF16), block_diag16(a_base))
    yield
    s = 2
    while s < GDN_BASE:
        ap16 = block_diag16(a_pow)
        upd = _dot(t_side.astype(BF16), ap16)
        s *= 2
        if s < GDN_BASE:
            a_pow = _dot(a_pow.astype(BF16), ap16)
        yield
        t_side = t_side + upd
    while s < A_CHUNK:
        if reverse:
            off_diag = jnp.logical_and((rs // s) % 2 == 0, cs // s == rs // s + 1)
        else:
            off_diag = jnp.logical_and((rs // s) % 2 == 1, cs // s == rs // s - 1)
        a21 = jnp.where(off_diag, a_side, 0.0)
        a21_t1 = _dot(a21.astype(BF16), block_diag16(t_side))
        yield
        upd = _dot(t_side.astype(BF16), block_diag16(a21_t1))
        yield
        t_side = t_side - upd
        s *= 2

    eg = jnp.exp(gc)
    uw = _dot(block_diag16(t_side), jnp.concatenate([v * beta, kb * eg], axis=1).astype(BF16))
    yield
    u = uw[:, :A_DV]
    w = uw[:, A_DV:]
    qd = q * eg

    state = s_ref[hh]
    n_chunks = n // A_CHUNK
    order = range(n_chunks - 1, -1, -1) if reverse else range(n_chunks)
    for c in order:
        r0 = c * A_CHUNK
        rows = slice(r0, r0 + A_CHUNK)
        last = r0 if reverse else r0 + A_CHUNK - 1
        g_last = gc[last:last + 1, :]
        s16 = state.astype(BF16)
        wq = _dot(jnp.concatenate([w[rows], qd[rows]], axis=0).astype(BF16), s16)
        yield
        v_new = u[rows] - wq[:A_CHUNK]
        vn16 = v_new.astype(BF16)
        intra = _dot(attn16[rows, r0:r0 + A_CHUNK], vn16)
        kd = k[rows] * jnp.exp(g_last - gc[rows])
        s_add = _dot_tn(kd.astype(BF16), vn16)
        yield
        o_ref[0, rows, hcols] = (wq[A_CHUNK:] + intra).astype(BF16)
        state = state * jnp.exp(g_last) + s_add
    s_ref[hh] = state


def _gdn_core_kernel(qf_ref, kf_ref, vf_ref, gcf_ref, grf_ref, qb_ref, kb_ref, vb_ref, gcb_ref, grb_ref,
                     of_ref, ob_ref, sf_ref, sb_ref):
    @pl.when(pl.program_id(2) == 0)
    def _():
        sf_ref[...] = jnp.zeros_like(sf_ref)
        sb_ref[...] = jnp.zeros_like(sb_ref)

    chains = []
    for hh in range(GDN_HPB):
        head = hh if GDN_HPB == A_HEADS else pl.program_id(1) * GDN_HPB + hh
        chains.append(_gdn_direction(qf_ref, kf_ref, vf_ref, gcf_ref, grf_ref, sf_ref, of_ref, hh, head, False))
        chains.append(_gdn_direction(qb_ref, kb_ref, vb_ref, gcb_ref, grb_ref, sb_ref, ob_ref, hh, head, True))
    while chains:
        chains = [c for c in chains if next(c, True) is None]


def gdn_core(q, k, v, gcol, grow):
    bsz, seq, _ = q.shape
    n = GDN_BLK
    nb = seq // n
    w = GDN_HPB * A_DK
    fwd = lambda b, h, j: (b, j, h)
    bwd = lambda b, h, j: (b, nb - 1 - j, h)
    specs = []
    for im in (fwd, bwd):
        specs += [pl.BlockSpec((1, n, w), im), pl.BlockSpec((1, n, w), im), pl.BlockSpec((1, n, w), im),
                  pl.BlockSpec((1, n, GATE_LANES), (lambda im: lambda b, h, j: (b, im(b, h, j)[1], 0))(im)),
                  pl.BlockSpec((1, 4 * A_HEADS, n), (lambda im: lambda b, h, j: (b, 0, im(b, h, j)[1]))(im))]
    out = jax.ShapeDtypeStruct((bsz, seq, A_V), BF16)
    return pl.pallas_call(
        _gdn_core_kernel,
        grid=(bsz, A_HEADS // GDN_HPB, nb),
        in_specs=specs,
        out_specs=[pl.BlockSpec((1, n, w), fwd), pl.BlockSpec((1, n, w), bwd)],
        out_shape=[out, out],
        scratch_shapes=[pltpu.VMEM((GDN_HPB, A_DK, A_DV), F32), pltpu.VMEM((GDN_HPB, A_DK, A_DV), F32)],
        compiler_params=_params("parallel", "parallel", "arbitrary"),
        name="gdn_core",
    )(q, k, v, gcol, grow, q, k, v, gcol, grow)


def _gdn_tail(of_ref, ob_ref, z_ref, x_ref, nw_ref, wo_ref, g_ref):
    nw = nw_ref[...]
    parts = []
    for h in range(A_HEADS):
        cols = slice(h * A_DV, (h + 1) * A_DV)
        o = of_ref[:, cols].astype(F32) + ob_ref[:, cols].astype(F32)
        o = o * lax.rsqrt(jnp.mean(o * o, axis=-1, keepdims=True) + EPS) * nw
        parts.append((o * _silu(z_ref[:, cols].astype(F32))).astype(BF16))
    hout = _dot(jnp.concatenate(parts, axis=1), wo_ref[...])
    return x_ref[...] + _rms(hout, g_ref[...])


def gdn_mixer_block(x, g_pre, g_post, w_in, conv_w, a_log, dt_bias, norm_w, w_out, ffn_args):
    bsz, seq, _ = x.shape
    n_qkv = 2 * A_QK + A_V
    wqkv = w_in[:, :n_qkv].astype(BF16)
    wz = w_in[:, n_qkv:n_qkv + A_V].astype(BF16)
    pad = GATE_LANES - 4 * A_HEADS
    wab = jnp.pad(w_in[:, n_qkv + A_V:], ((0, 0), (0, pad))).astype(BF16)
    zeros = jnp.zeros((2, A_HEADS), F32)
    alog_row = jnp.pad(jnp.stack([a_log, zeros], axis=1).reshape(1, 4 * A_HEADS), ((0, 0), (0, pad)))
    dtb_row = jnp.pad(jnp.stack([dt_bias, zeros], axis=1).reshape(1, 4 * A_HEADS), ((0, 0), (0, pad)))
    q, k, v, z, gcol, grow = gdn_project(x, g_pre.reshape(1, -1), wqkv, wz, wab, conv_w, alog_row, dtb_row)
    o_f, o_b = gdn_core(q, k, v, gcol, grow)
    t = bsz * seq
    args = (o_f.reshape(t, A_V), o_b.reshape(t, A_V), z.reshape(t, A_V), x.reshape(t, D_MODEL),
            norm_w.reshape(1, A_DV), w_out.astype(BF16), g_post.reshape(1, -1))
    row = functools.partial(_tail_row_spec, n_rows=t)
    specs = [row(A_V), row(A_V), row(A_V), row(D_MODEL), _const_spec((1, A_DV)),
             _const_spec((A_V, D_MODEL)), _const_spec((1, D_MODEL))]
    return tail_ffn_block(_gdn_tail, args, specs, ffn_args, "gdn_tail_ffn").reshape(bsz, seq, D_MODEL)


RET_BLK = 256
DECAY_ROWS = SUBLANE


def _log_gamma(dl_ref):
    return jnp.log1p(-jnp.exp2(dl_ref[...]))


def _ret_proj_kernel(x_ref, g_ref, w_ref, cos_ref, sin_ref, dl_ref,
                     q_ref, qf_ref, qb_ref, k_ref, kf_ref, kb_ref, v_ref, gate_ref, *, tm):
    xn = _rms(x_ref[0], g_ref[...]).astype(BF16)
    cos = cos_ref[...]
    sin = sin_ref[...]
    nq = B_HEADS * B_DK
    nv = B_HEADS * B_DV
    half = B_DK // 2
    n = RET_BLK
    log_gamma = _log_gamma(dl_ref)
    pos = (lax.broadcasted_iota(jnp.int32, (tm, half), 0) % n).astype(F32)
    for part in range(2):
        p = _dot(xn, w_ref[:, part * nq:(part + 1) * nq])
        for h in range(B_HEADS):
            lgf = log_gamma[0:1, h:h + 1]
            lgb = log_gamma[1:2, h:h + 1]
            x1 = p[:, h * B_DK:h * B_DK + half]
            x2 = p[:, h * B_DK + half:(h + 1) * B_DK]
            if part == 0:
                variants = ((q_ref, None), (qf_ref, jnp.exp(lgf * (pos + 1.0))), (qb_ref, jnp.exp(lgb * (n - pos))))
                scale = 1.0
            else:
                variants = ((k_ref, None), (kf_ref, jnp.exp(lgf * (n - 1.0 - pos))), (kb_ref, jnp.exp(lgb * pos)))
                scale = B_DK ** -0.5
            r1 = (x1 * cos - x2 * sin) * scale
            r2 = (x2 * cos + x1 * sin) * scale
            for ref, fac in variants:
                ref[0, :, h * B_DK:h * B_DK + half] = (r1 if fac is None else r1 * fac).astype(BF16)
                ref[0, :, h * B_DK + half:(h + 1) * B_DK] = (r2 if fac is None else r2 * fac).astype(BF16)
    v_ref[0] = _dot(xn, w_ref[:, 2 * nq:2 * nq + nv]).astype(BF16)
    gate_ref[0] = _dot(xn, w_ref[:, 2 * nq + nv:]).astype(BF16)


def ret_project(x, g_pre, w_in, cos, sin, decay_logit_tile):
    bsz, seq, _ = x.shape
    tm = PROJ_TM
    nq = B_HEADS * B_DK
    nv = B_HEADS * B_DV
    act = lambda c: jax.ShapeDtypeStruct((bsz, seq, c), BF16)
    act_spec = lambda c: pl.BlockSpec((1, tm, c), lambda b, i: (b, i, 0))
    tab_spec = pl.BlockSpec((tm, B_DK // 2), lambda b, i: (i, 0))
    return pl.pallas_call(
        functools.partial(_ret_proj_kernel, tm=tm),
        grid=(bsz, seq // tm),
        in_specs=[act_spec(D_MODEL), _const_spec((1, D_MODEL)), _const_spec((D_MODEL, 2 * nq + 2 * nv)),
                  tab_spec, tab_spec, _const_spec((DECAY_ROWS, LANE))],
        out_specs=[act_spec(nq)] * 6 + [act_spec(nv), act_spec(nv)],
        out_shape=[act(nq)] * 6 + [act(nv), act(nv)],
        compiler_params=_params("parallel", "parallel"),
        name="ret_project",
    )(x, g_pre, w_in, cos, sin, decay_logit_tile)


def _ret_forward(q_ref, k_ref, qx_ref, kz_ref, v_ref, o_ref, r_ref, mask_ref, h, g_blk):
    qcols = slice(h * B_DK, (h + 1) * B_DK)
    vcols = slice(h * B_DV, (h + 1) * B_DV)
    v = v_ref[0, :, vcols]
    scores = _dot_nt(q_ref[0, :, qcols], k_ref[0, :, qcols])
    r = r_ref[h]
    cross = _dot(qx_ref[0, :, qcols], r.astype(BF16))
    r_add = _dot_tn(kz_ref[0, :, qcols], v)
    yield
    inner = _dot((scores * mask_ref[h]).astype(BF16), v)
    r_ref[h] = r * g_blk + r_add
    yield
    o_ref[0, :, vcols] = (inner + cross).astype(BF16)


def _ret_backward(qx_ref, kz_ref, v_ref, o_ref, r_ref, h, g_blk):
    qcols = slice(h * B_DK, (h + 1) * B_DK)
    vcols = slice(h * B_DV, (h + 1) * B_DV)
    r = r_ref[h]
    cross = _dot(qx_ref[0, :, qcols], r.astype(BF16))
    r_add = _dot_tn(kz_ref[0, :, qcols], v_ref[0, :, vcols])
    yield
    o_ref[0, :, vcols] = cross.astype(BF16)
    r_ref[h] = r * g_blk + r_add


def _ret_core_kernel(dl_ref, q_ref, k_ref, qxf_ref, kzf_ref, vf_ref, qxb_ref, kzb_ref, vb_ref,
                     of_ref, ob_ref, rf_ref, rb_ref, mask_ref):
    n = RET_BLK
    log_gamma = _log_gamma(dl_ref)

    @pl.when(pl.program_id(1) == 0)
    def _():
        rf_ref[...] = jnp.zeros_like(rf_ref)
        rb_ref[...] = jnp.zeros_like(rb_ref)
        ri = lax.broadcasted_iota(jnp.int32, (n, n), 0)
        ci = lax.broadcasted_iota(jnp.int32, (n, n), 1)
        dist = (ri - ci).astype(F32)
        for h in range(B_HEADS):
            lgf = log_gamma[0:1, h:h + 1]
            lgb = log_gamma[1:2, h:h + 1]
            mask_ref[h] = (jnp.where(dist >= 0, jnp.exp(lgf * jnp.maximum(dist, 0.0)), 0.0)
                           + jnp.where(dist <= 0, jnp.exp(lgb * jnp.maximum(-dist, 0.0)), 0.0))

    g_blk = jnp.exp(log_gamma * float(n))
    chains = []
    for h in range(B_HEADS):
        chains.append(_ret_forward(q_ref, k_ref, qxf_ref, kzf_ref, vf_ref, of_ref, rf_ref, mask_ref, h,
                                   g_blk[0:1, h:h + 1]))
        chains.append(_ret_backward(qxb_ref, kzb_ref, vb_ref, ob_ref, rb_ref, h, g_blk[1:2, h:h + 1]))
    while chains:
        chains = [c for c in chains if next(c, True) is None]


def ret_core(decay_logit_tile, q, qxf, qxb, k, kzf, kzb, v):
    bsz, seq, _ = q.shape
    n = RET_BLK
    nb = seq // n
    nq = B_HEADS * B_DK
    nv = B_HEADS * B_DV
    fwd = lambda b, j: (b, j, 0)
    bwd = lambda b, j: (b, nb - 1 - j, 0)
    qk = lambda im: pl.BlockSpec((1, n, nq), im)
    val = lambda im: pl.BlockSpec((1, n, nv), im)
    out = jax.ShapeDtypeStruct((bsz, seq, nv), BF16)
    state = pltpu.VMEM((B_HEADS, B_DK, B_DV), F32)
    return pl.pallas_call(
        _ret_core_kernel,
        grid=(bsz, nb),
        in_specs=[_const_spec((DECAY_ROWS, LANE)), qk(fwd), qk(fwd), qk(fwd), qk(fwd), val(fwd),
                  qk(bwd), qk(bwd), val(bwd)],
        out_specs=[val(fwd), val(bwd)],
        out_shape=[out, out],
        scratch_shapes=[state, state, pltpu.VMEM((B_HEADS, n, n), F32)],
        compiler_params=_params("parallel", "arbitrary"),
        name="ret_core",
    )(decay_logit_tile, q, k, qxf, kzf, v, qxb, kzb, v)


def _ret_tail(of_ref, ob_ref, gate_ref, x_ref, gn_ref, wo_ref, g_ref):
    parts = []
    for h in range(B_HEADS):
        cols = slice(h * B_DV, (h + 1) * B_DV)
        o = of_ref[:, cols].astype(F32) + ob_ref[:, cols].astype(F32)
        oc = o - jnp.mean(o, axis=-1, keepdims=True)
        on = oc * lax.rsqrt(jnp.mean(oc * oc, axis=-1, keepdims=True) + EPS) * gn_ref[:, cols]
        parts.append((_silu(gate_ref[:, cols].astype(F32)) * on).astype(BF16))
    hout = _dot(jnp.concatenate(parts, axis=1), wo_ref[...])
    return x_ref[...] + _rms(hout, g_ref[...])


def _rope_tables(seq):
    inv = ROPE_BASE ** (-np.arange(0, B_DK, 2, dtype=np.float64) / B_DK)
    ang = np.arange(seq, dtype=np.float64)[:, None] * inv[None, :]
    return jnp.asarray(np.cos(ang), F32), jnp.asarray(np.sin(ang), F32)


def retnet_mixer_block(x, g_pre, g_post, w_in, decay_logit, gn_w, w_out, ffn_args):
    bsz, seq, _ = x.shape
    cos, sin = _rope_tables(seq)
    dl = jnp.pad(decay_logit, ((0, DECAY_ROWS - 2), (0, LANE - B_HEADS)), constant_values=-1.0)
    q, qxf, qxb, k, kzf, kzb, v, gate = ret_project(x, g_pre.reshape(1, -1), w_in.astype(BF16), cos, sin, dl)
    o_f, o_b = ret_core(dl, q, qxf, qxb, k, kzf, kzb, v)
    t = bsz * seq
    nv = B_HEADS * B_DV
    args = (o_f.reshape(t, nv), o_b.reshape(t, nv), gate.reshape(t, nv), x.reshape(t, D_MODEL),
            gn_w.reshape(1, nv), w_out.astype(BF16), g_post.reshape(1, -1))
    row = functools.partial(_tail_row_spec, n_rows=t)
    specs = [row(nv), row(nv), row(nv), row(D_MODEL), _const_spec((1, nv)),
             _const_spec((nv, D_MODEL)), _const_spec((1, D_MODEL))]
    return tail_ffn_block(_ret_tail, args, specs, ffn_args, "ret_tail_ffn").reshape(bsz, seq, D_MODEL)


DFT_N2 = 128
HY_LANES = 4 * D_MODEL
HY_CB = D_MODEL
HY_PAIRS = 2
FILTER_TL = 256


def _dft_tables(seq):
    n = 2 * seq
    n2 = DFT_N2
    n1 = n // n2
    kk = np.arange(n1, dtype=np.float64)[:, None]
    nn = np.arange(n1 // 2, dtype=np.float64)[None, :]
    ang = 2.0 * np.pi * kk * nn / n1
    cc, ss = np.cos(ang), np.sin(ang)
    m1 = np.block([[cc, ss], [-ss, cc]])
    m1_real = np.concatenate([cc, -ss], axis=0)
    m3 = m1.T / n
    k1 = np.arange(n1, dtype=np.float64)[:, None, None]
    k2 = np.arange(n2, dtype=np.float64)[None, :, None]
    t2 = np.arange(n2, dtype=np.float64)[None, None, :]
    theta = 2.0 * np.pi * (t2 * k2 / n2 + t2 * k1 / n)
    c, s = np.cos(theta), np.sin(theta)
    m2f = np.concatenate([np.concatenate([c, s], axis=2), np.concatenate([-s, c], axis=2)], axis=1)
    m2i = np.transpose(m2f, (0, 2, 1))
    cast = lambda a: jnp.asarray(a, F32).astype(BF16)
    return n1, cast(m1), cast(m1_real), cast(m3), cast(m2f), cast(m2i)


def _hy_proj_kernel(xm_ref, xp_ref, xn_ref, g_ref, w_ref, bin_ref, cw_ref, cb_ref,
                    v_ref, g1_ref, g2_ref, *, tm, n_tiles):
    xne = _normed_with_halo(xm_ref, xp_ref, xn_ref, g_ref[...])
    for part, ref in enumerate((v_ref, g1_ref, g2_ref)):
        cols = slice(part * D_MODEL, (part + 1) * D_MODEL)
        p = _zero_outside(_dot(xne, w_ref[:, cols]) + bin_ref[:, cols], tm, n_tiles)
        ref[0] = (_conv3(p, cw_ref[:, cols], tm) + cb_ref[:, cols]).astype(BF16)


def hy_project(x, g_pre, w_in, b_in, conv_w, conv_b):
    bsz, seq, _ = x.shape
    tm = PROJ_TM
    nt = seq // tm
    out = jax.ShapeDtypeStruct((bsz, seq, D_MODEL), BF16)
    spec = pl.BlockSpec((1, tm, D_MODEL), lambda b, i: (b, i, 0))
    return pl.pallas_call(
        functools.partial(_hy_proj_kernel, tm=tm, n_tiles=nt),
        grid=(bsz, nt),
        in_specs=_halo_specs(tm, seq) + [
            _const_spec((1, D_MODEL)), _const_spec((D_MODEL, 3 * D_MODEL)), _const_spec((1, 3 * D_MODEL)),
            _const_spec((3, 3 * D_MODEL)), _const_spec((1, 3 * D_MODEL))],
        out_specs=[spec, spec, spec],
        out_shape=[out, out, out],
        compiler_params=_params("parallel", "parallel"),
        name="hy_project",
    )(x, x, x, g_pre, w_in, b_in, conv_w, conv_b)


def _dot_f32(a, b):
    return jnp.dot(a, b, preferred_element_type=F32, precision=lax.Precision.HIGHEST)


def _hy_filter_kernel(f_ref, w1t_ref, w1c_ref, w1s_ref, b1_ref, fr1_ref, w2_ref, b2_ref, fr2_ref, w3_ref,
                      delta_ref, h_ref, *, tl, seq):
    n = (lax.broadcasted_iota(jnp.int32, (tl, 1), 0) + pl.program_id(0) * tl).astype(F32)
    t = n / (seq - 1.0)
    arg = (2.0 * math.pi / seq) * n * f_ref[...]
    pre = (t * w1t_ref[...] + _dot_f32(jnp.cos(arg), w1c_ref[...]) + _dot_f32(-jnp.sin(arg), w1s_ref[...])
           + b1_ref[...])
    h1 = jnp.sin(fr1_ref[...] * pre)
    h2 = jnp.sin(fr2_ref[...] * (_dot_f32(h1, w2_ref[...]) + b2_ref[...]))
    h3 = _dot_f32(h2, w3_ref[...])
    window = jnp.exp(-t * delta_ref[...])
    for s in range(4):
        hs = h3[:, s * D_MODEL:(s + 1) * D_MODEL] * window
        if s >= 2:
            hs = jnp.where(n == 0.0, 0.0, hs)
        h_ref[s] = hs


def hy_filters(seq, f_w1, f_b1, f_fr1, f_w2, f_b2, f_fr2, f_w3):
    bands = (C_EMB - 1) // 2
    fw = C_FILTER_WIDTH
    f = np.zeros((1, LANE), np.float32)
    f[0, :bands] = np.linspace(1e-4, bands - 1, bands, dtype=np.float32)
    deltas = np.abs(np.linspace(C_MIN_DECAY, C_MAX_DECAY, D_MODEL, dtype=np.float32))[None, :]
    pad = ((0, LANE - bands), (0, 0))
    w1c = jnp.pad(f_w1[1:1 + bands], pad)
    w1s = jnp.pad(f_w1[1 + bands:], pad)
    tl = FILTER_TL
    row = lambda a: a.reshape(1, -1)
    args = (jnp.asarray(f), f_w1[0:1], w1c, w1s, row(f_b1), row(f_fr1), f_w2, row(f_b2), row(f_fr2), f_w3,
            jnp.asarray(deltas))
    return pl.pallas_call(
        functools.partial(_hy_filter_kernel, tl=tl, seq=seq),
        grid=(seq // tl,),
        in_specs=[_const_spec(a.shape) for a in args],
        out_specs=pl.BlockSpec((4, tl, D_MODEL), lambda i: (0, i, 0)),
        out_shape=jax.ShapeDtypeStruct((4, seq, D_MODEL), F32),
        compiler_params=_params("parallel"),
        name="hy_filters",
    )(*args)


def _dft1_kernel(u_ref, m1_ref, y_ref):
    p, half, lw = u_ref.shape[1:]
    y_ref[0] = _dot(m1_ref[...], u_ref[0].reshape(p * half, lw).astype(BF16)).astype(BF16)


def dft_stage1(u4, m1):
    g, p, half, width = u4.shape
    rows = m1.shape[0]
    lw = HY_LANES
    return pl.pallas_call(
        _dft1_kernel,
        grid=(g, width // lw),
        in_specs=[pl.BlockSpec((1, p, half, lw), lambda b, j: (b, 0, 0, j)), _const_spec(m1.shape)],
        out_specs=pl.BlockSpec((1, rows, lw), lambda b, j: (b, 0, j)),
        out_shape=jax.ShapeDtypeStruct((g, rows, width), BF16),
        compiler_params=_params("parallel", "parallel"),
        name="dft_stage1",
    )(u4, m1)


def _filter_spec_kernel(y_ref, m2f_ref, h_ref):
    n2, cb = y_ref.shape[3:]
    x = [_dot(m2f_ref[0], y_ref[s, :, 0].reshape(2 * n2, cb)) for s in range(4)]
    for order in range(2):
        fwd, rev = x[order], x[2 + order]
        h_ref[order, 0, 0] = fwd[:n2] + rev[:n2]
        h_ref[order, 1, 0] = fwd[n2:] - rev[n2:]


def filter_spectrum(y5, m2f):
    s, _, n1, n2, c = y5.shape
    cb = HY_CB
    return pl.pallas_call(
        _filter_spec_kernel,
        grid=(n1, c // cb),
        in_specs=[pl.BlockSpec((s, 2, 1, n2, cb), lambda k, j: (0, 0, k, 0, j)),
                  pl.BlockSpec((1, 2 * n2, 2 * n2), lambda k, j: (k, 0, 0))],
        out_specs=pl.BlockSpec((2, 2, 1, n2, cb), lambda k, j: (0, 0, k, 0, j)),
        out_shape=jax.ShapeDtypeStruct((2, 2, n1, n2, c), F32),
        compiler_params=_params("parallel", "parallel"),
        name="filter_spectrum",
    )(y5, m2f)


def _conv_freq_kernel(y_ref, m2f_ref, m2i_ref, h_ref, z_ref):
    n2, cb = y_ref.shape[3:]
    hr = h_ref[0, 0, 0]
    hi = h_ref[0, 1, 0]
    for p in range(HY_PAIRS):
        x = _dot(m2f_ref[0], y_ref[p, :, 0].reshape(2 * n2, cb))
        xr, xi = x[:n2], x[n2:]
        prod = jnp.concatenate([xr * hr - xi * hi, xr * hi + xi * hr], axis=0).astype(BF16)
        z_ref[p, :, 0] = _dot(m2i_ref[0], prod).reshape(2, n2, cb).astype(BF16)


def conv_freq(y5, m2f, m2i, hspec, order):
    g, _, n1, n2, c = y5.shape
    cb = HY_CB
    blk = pl.BlockSpec((HY_PAIRS, 2, 1, n2, cb), lambda k, j, b: (b, 0, k, 0, j))
    tab = pl.BlockSpec((1, 2 * n2, 2 * n2), lambda k, j, b: (k, 0, 0))
    hblk = pl.BlockSpec((1, 2, 1, n2, cb), lambda k, j, b: (order, 0, k, 0, j))
    return pl.pallas_call(
        _conv_freq_kernel,
        grid=(n1, c // cb, g // HY_PAIRS),
        in_specs=[blk, tab, tab, hblk],
        out_specs=blk,
        out_shape=jax.ShapeDtypeStruct(y5.shape, BF16),
        compiler_params=_params("parallel", "parallel", "parallel"),
        name="conv_freq",
    )(y5, m2f, m2i, hspec)


def _conv_time_kernel(z_ref, u_ref, gate_ref, skip_ref, m3_ref, *rest, chain):
    half, lw = u_ref.shape[2:]
    y = _dot(m3_ref[...], z_ref[0])
    u = u_ref[0].reshape(2 * half, lw).astype(F32)
    out = (gate_ref[0].reshape(2 * half, lw).astype(F32) * (y + skip_ref[...] * u)).astype(BF16)
    if chain:
        m1_ref, o_ref, y_ref = rest
        y_ref[0] = _dot(m1_ref[...], out).astype(BF16)
    else:
        (o_ref,) = rest
    o_ref[0] = out.reshape(2, half, lw)


def conv_time(z3, u4, gate4, skip, m3, m1=None):
    g, _, half, width = u4.shape
    rows = z3.shape[1]
    lw = HY_LANES
    chain = m1 is not None
    sig = pl.BlockSpec((1, 2, half, lw), lambda b, j: (b, 0, 0, j))
    spec = pl.BlockSpec((1, rows, lw), lambda b, j: (b, 0, j))
    in_specs = [spec, sig, sig, _const_spec((1, lw)), _const_spec(m3.shape)]
    out_specs = [sig]
    out_shape = [jax.ShapeDtypeStruct(u4.shape, BF16)]
    args = [z3, u4, gate4, skip, m3]
    if chain:
        in_specs.append(_const_spec(m1.shape))
        out_specs.append(spec)
        out_shape.append(jax.ShapeDtypeStruct(z3.shape, BF16))
        args.append(m1)
    res = pl.pallas_call(
        functools.partial(_conv_time_kernel, chain=chain),
        grid=(g, width // lw),
        in_specs=in_specs,
        out_specs=out_specs,
        out_shape=out_shape,
        compiler_params=_params("parallel", "parallel"),
        name="conv_time_chain" if chain else "conv_time",
    )(*args)
    return res if chain else res[0]


def _hy_tail(z_ref, x_ref, wo_ref, bo_ref, g_ref):
    hout = _dot(z_ref[...], wo_ref[...]) + bo_ref[...]
    return x_ref[...] + _rms(hout, g_ref[...])


def hyena_mixer_block(x, g_pre, g_post, w_in, b_in, conv_w, conv_b, f_w1, f_b1, f_fr1, f_w2, f_b2, f_fr2, f_w3,
                      bias_d, w_out, b_out, ffn_args):
    bsz, seq, _ = x.shape
    n1, m1, m1_real, m3, m2f, m2i = _dft_tables(seq)
    n2 = DFT_N2
    width = n2 * D_MODEL
    pairs = bsz // 2
    sig4 = lambda a: a.reshape(pairs, 2, n1 // 2, width)
    five = lambda a: a.reshape(a.shape[0], 2, n1, n2, D_MODEL)

    filt = hy_filters(seq, f_w1, f_b1, f_fr1, f_w2, f_b2, f_fr2, f_w3)
    yield
    hspec = filter_spectrum(five(dft_stage1(filt.reshape(4, 1, n1 // 2, width), m1_real)), m2f)
    yield
    v, g1, g2 = hy_project(x, g_pre.reshape(1, -1), w_in.astype(BF16), b_in.reshape(1, -1), conv_w,
                           conv_b.reshape(1, -1))
    yield
    v4, g14, g24 = sig4(v), sig4(g1), sig4(g2)
    y = dft_stage1(v4, m1)
    yield
    z = conv_freq(five(y), m2f, m2i, hspec, 0).reshape(pairs, 2 * n1, width)
    yield
    skip = jnp.tile(bias_d, (1, HY_LANES // D_MODEL))
    z1, y = conv_time(z, v4, g14, skip[0:1], m3, m1)
    yield
    z = conv_freq(five(y), m2f, m2i, hspec, 1).reshape(pairs, 2 * n1, width)
    yield
    z2 = conv_time(z, z1, g24, skip[1:2], m3)
    yield
    t = bsz * seq
    args = (z2.reshape(t, D_MODEL), x.reshape(t, D_MODEL), w_out.astype(BF16), b_out.reshape(1, -1),
            g_post.reshape(1, -1))
    row = functools.partial(_tail_row_spec, n_rows=t)
    specs = [row(D_MODEL), row(D_MODEL), _const_spec((D_MODEL, D_MODEL)), _const_spec((1, D_MODEL)),
             _const_spec((1, D_MODEL))]
    yield tail_ffn_block(_hy_tail, args, specs, ffn_args, "hy_tail_ffn").reshape(bsz, seq, D_MODEL)


def kernel(x_prompt, x_sample, norm_g, ffn_w_gate, ffn_w_up, ffn_w_down, a_w_in, a_conv_w, a_a_log, a_dt_bias,
           a_norm_w, a_w_out, b_w_in, b_decay_logit, b_gn_w, b_w_out, c_w_in, c_b_in, c_conv_w, c_conv_b,
           c_f_w1, c_f_b1, c_f_freq1, c_f_w2, c_f_b2, c_f_freq2, c_f_w3, c_bias_d, c_w_out, c_b_out):
    depth = norm_g.shape[0]
    wg = ffn_w_gate.astype(BF16)
    wu = ffn_w_up.astype(BF16)
    wd = ffn_w_down.astype(BF16)

    def ffn_args(i, which):
        return (norm_g[i, 4 * which:4 * which + 2], wg[i, which], wu[i, which], wd[i, which])

    def mixers_then_ffn(xs, i):
        kind, j = i % 3, i // 3
        g_pre, g_post = norm_g[i, 2], norm_g[i, 3]
        ffn = ffn_args(i, 1)
        if kind == 0:
            return [gdn_mixer_block(x, g_pre, g_post, a_w_in[j], a_conv_w[j], a_a_log[j], a_dt_bias[j],
                                    a_norm_w[j], a_w_out[j], ffn) for x in xs]
        if kind == 1:
            return [retnet_mixer_block(x, g_pre, g_post, b_w_in[j], b_decay_logit[j], b_gn_w[j], b_w_out[j], ffn)
                    for x in xs]
        return _in_lockstep([
            hyena_mixer_block(x, g_pre, g_post, c_w_in[j], c_b_in[j], c_conv_w[j], c_conv_b[j], c_f_w1[j], c_f_b1[j],
                              c_f_freq1[j], c_f_w2[j], c_f_b2[j], c_f_freq2[j], c_f_w3[j], c_bias_d[j], c_w_out[j],
                              c_b_out[j], ffn) for x in xs])

    xs = [x_prompt, x_sample]
    for i in range(depth):
        xs = [ffn_block(x.reshape(-1, D_MODEL), *ffn_args(i, 0)).reshape(x.shape) for x in xs]
        xs = mixers_then_ffn(xs, i)
    return tuple(xs)
```
